```python
import jax
import jax.numpy as jnp
from jax import lax
import numpy as np

D_MODEL = 4096
BATCH = 1
SEQ = 8192
DEPTH = 4

GRID_W = 64
CTX_LEN = 256
N_MIXERS = 3
N_MOD = 6
ADA_RANK = 512
ADA_INIT = 0.2
MLP_HIDDEN = 4 * D_MODEL
NA_HEADS = 32
NA_HEAD_DIM = D_MODEL // NA_HEADS
NA_KH = 8
NA_KW = 16
WA_HEADS = 32
WA_KV_HEADS = 8
WA_HEAD_DIM = D_MODEL // WA_HEADS
WA_WINDOW = 128
ROPE_BASE = 10000.0
SG_WIDTH = D_MODEL
SG_GROUPS = 8
SG_CHUNK = 128

EPS = 1e-6
NEG_INF = -1e30

kernel_name = 'hybrid_na_swa_gmlp_dit_trunk'


def rmsnorm(x, g):
    xf = x.astype(jnp.float32)
    y = xf * lax.rsqrt(jnp.mean(xf * xf, axis=-1, keepdims=True) + EPS)
    return (y * g.astype(jnp.float32)).astype(x.dtype)


def ada_modulation(cond, a, b, bias):
    m = (jax.nn.silu(cond) @ a) @ b + bias
    return jnp.split(m, N_MOD, axis=-1)


def modulate(h, shift, scale):
    return h * (1.0 + scale) + shift


def sq_relu_mlp(h, w1, w2):
    return jnp.square(jax.nn.relu(h @ w1)) @ w2


def axial_rope(n_tokens, head_dim):
    t = jnp.arange(n_tokens, dtype=jnp.int32)
    row = (t // GRID_W).astype(jnp.float32)
    col = (t % GRID_W).astype(jnp.float32)
    axis_dim = head_dim // 2
    inv = ROPE_BASE ** (-jnp.arange(0, axis_dim, 2, dtype=jnp.float32) / axis_dim)
    ar = row[:, None] * inv
    ac = col[:, None] * inv
    ang = jnp.concatenate([ar, ar, ac, ac], axis=-1)
    return jnp.cos(ang), jnp.sin(ang)


def rotate_axial(x):
    x1, x2, x3, x4 = jnp.split(x, 4, axis=-1)
    return jnp.concatenate([-x2, x1, -x4, x3], axis=-1)


def apply_rope(x, cos, sin):
    cos = cos[:, None, :].astype(x.dtype)
    sin = sin[:, None, :].astype(x.dtype)
    return x * cos + rotate_axial(x) * sin


def context_self_attention(q, k, v, sink=None):
    B, L, H, dh = q.shape
    kvh = k.shape[2]
    qg = q.reshape(B, L, kvh, H // kvh, dh)
    s = jnp.einsum('bqkgd,bjkd->bkgqj', qg, k).astype(jnp.float32)
    if sink is not None:
        s_sink = jnp.broadcast_to(sink.reshape(kvh, H // kvh, 1, 1).astype(jnp.float32), s.shape[:-1] + (1,))
        s = jnp.concatenate([s, s_sink], axis=-1)
    p = jax.nn.softmax(s, axis=-1)[..., :L].astype(v.dtype)
    return jnp.einsum('bkgqj,bjkd->bqkgd', p, v).reshape(B, L, H * dh)


def neighborhood_attention(hx, hc, w_qkv, w_o, q_gain, k_gain, rel_bias, with_ctx_out):
    B, S, _ = hx.shape
    rows = S // GRID_W
    kh = min(NA_KH, rows)
    H, dh = NA_HEADS, NA_HEAD_DIM
    scale = dh ** -0.5

    def project(h):
        qkv = (h @ w_qkv).reshape(B, h.shape[1], 3, H, dh)
        return rmsnorm(qkv[:, :, 0], q_gain) * scale, rmsnorm(qkv[:, :, 1], k_gain), qkv[:, :, 2]

    q, k, v = project(hx)
    qc, kc, vc = project(hc)

    ncb = GRID_W // NA_KW
    q_cols = np.arange(GRID_W).reshape(ncb, NA_KW)
    blk_start = np.clip(np.arange(ncb) * NA_KW - NA_KW // 2, 0, GRID_W - 2 * NA_KW)
    key_cols = blk_start[:, None] + np.arange(2 * NA_KW)
    win_start = np.clip(q_cols - NA_KW // 2, 0, GRID_W - NA_KW)
    kcb = key_cols[:, None, :]
    col_ok = (kcb >= win_start[..., None]) & (kcb < win_start[..., None] + NA_KW)
    dc_idx = np.clip(kcb - q_cols[..., None], -(NA_KW - 1), NA_KW - 1) + NA_KW - 1
    col_mask = jnp.asarray(col_ok)[None, None, :, :, None, :]

    kg = k.reshape(B, rows, GRID_W, H, dh)[:, :, key_cols]
    vg = v.reshape(B, rows, GRID_W, H, dh)[:, :, key_cols]
    qg = q.reshape(B, rows, ncb, NA_KW, H, dh)
    n_loc = kh * 2 * NA_KW

    def row_block(r):
        r0 = jnp.clip(r - kh // 2, 0, rows - kh)
        q_r = lax.dynamic_index_in_dim(qg, r, axis=1, keepdims=False)
        k_r = lax.dynamic_slice_in_dim(kg, r0, kh, axis=1)
        v_r = lax.dynamic_slice_in_dim(vg, r0, kh, axis=1)
        s_loc = jnp.einsum('bnqhd,binjhd->bhnqij', q_r, k_r).astype(jnp.float32)
        dr_idx = r0 + jnp.arange(kh) - r + NA_KH - 1
        bias = rel_bias[:, dr_idx][:, :, dc_idx]
        bias = bias.transpose(0, 2, 3, 1, 4).astype(jnp.float32)
        s_loc = jnp.where(col_mask, s_loc + bias, NEG_INF).reshape(B, H, ncb, NA_KW, n_loc)
        s_ctx = jnp.einsum('bnqhd,bchd->bhnqc', q_r, kc).astype(jnp.float32)
        p = jax.nn.softmax(jnp.concatenate([s_loc, s_ctx], axis=-1), axis=-1).astype(v.dtype)
        p_loc = p[..., :n_loc].reshape(B, H, ncb, NA_KW, kh, 2 * NA_KW)
        return (jnp.einsum('bhnqij,binjhd->bnqhd', p_loc, v_r)
                + jnp.einsum('bhnqc,bchd->bnqhd', p[..., n_loc:], vc))

    o = lax.map(row_block, jnp.arange(rows, dtype=jnp.int32))
    out_x = jnp.moveaxis(o, 0, 1).reshape(B, S, H * dh) @ w_o
    out_c = context_self_attention(qc, kc, vc) @ w_o if with_ctx_out else None
    return out_x, out_c


def window_attention(hx, hc, w_qkv, w_o, q_gain, k_gain, sink, cos, sin, with_ctx_out):
    B, S, _ = hx.shape
    H, KVH, dh, W = WA_HEADS, WA_KV_HEADS, WA_HEAD_DIM, WA_WINDOW
    G = H // KVH
    scale = dh ** -0.5

    def project(h):
        n = h.shape[1]
        p = h @ w_qkv
        q = p[..., :H * dh].reshape(B, n, H, dh)
        k = p[..., H * dh:(H + KVH) * dh].reshape(B, n, KVH, dh)
        v = p[..., (H + KVH) * dh:].reshape(B, n, KVH, dh)
        return rmsnorm(q, q_gain), rmsnorm(k, k_gain), v

    q, k, v = project(hx)
    q = apply_rope(q, cos, sin) * scale
    k = apply_rope(k, cos, sin)
    qc, kc, vc = project(hc)
    qc = qc * scale

    nblk = S // W

    def band(t):
        tp = jnp.pad(t, ((0, 0), (W, W), (0, 0), (0, 0))).reshape(B, nblk + 2, W, KVH, dh)
        return jnp.concatenate([tp[:, :-2], tp[:, 1:-1], tp[:, 2:]], axis=2)

    kb, vb = band(k), band(v)
    qb = q.reshape(B, nblk, W, KVH, G, dh)
    qi = np.arange(W)[:, None]
    kj = np.arange(3 * W)[None, :]
    key_pos = np.arange(nblk)[:, None, None] * W + kj[None] - W
    ok = (np.abs(kj - W - qi) <= W)[None] & (key_pos >= 0) & (key_pos < S)

    s_loc = jnp.einsum('bnqkgd,bnjkd->bkgnqj', qb, kb).astype(jnp.float32)
    s_loc = jnp.where(jnp.asarray(ok), s_loc, NEG_INF)
    s_ctx = jnp.einsum('bnqkgd,bckd->bkgnqc', qb, kc).astype(jnp.float32)
    s_sink = jnp.broadcast_to(sink.reshape(KVH, G, 1, 1, 1).astype(jnp.float32), (B, KVH, G, nblk, W, 1))
    p = jax.nn.softmax(jnp.concatenate([s_loc, s_ctx, s_sink], axis=-1), axis=-1).astype(v.dtype)
    L = kc.shape[1]
    o = (jnp.einsum('bkgnqj,bnjkd->bnqkgd', p[..., :3 * W], vb)
         + jnp.einsum('bkgnqc,bckd->bnqkgd', p[..., 3 * W:3 * W + L], vc))
    out_x = o.reshape(B, S, H * dh) @ w_o
    out_c = context_self_attention(qc, kc, vc, sink) @ w_o if with_ctx_out else None
    return out_x, out_c


def spatial_gating_mlp(hx, hc, w_in, v_gain, w_s, b_s, w_out, with_ctx_out):
    def sgu(h):
        B, n, _ = h.shape
        u, v = jnp.split(jax.nn.gelu(h @ w_in, approximate=False), 2, axis=-1)
        v = rmsnorm(v, v_gain).reshape(B, n // SG_CHUNK, SG_CHUNK, SG_GROUPS, SG_WIDTH // SG_GROUPS)
        sv = jnp.einsum('gpq,bnqgc->bnpgc', w_s, v) + b_s.T[:, :, None]
        return (u * sv.reshape(B, n, SG_WIDTH)) @ w_out

    out_x = sgu(hx)
    out_c = sgu(hc) if with_ctx_out else None
    return out_x, out_c


def setup_inputs(seed: int = 0) -> dict:
    key = jax.random.key(seed)
    ks = iter(jax.random.split(key, 32))

    def normal(shape, scale):
        return jax.random.normal(next(ks), shape, jnp.float32) * scale

    L = DEPTH
    n_a = len(range(0, DEPTH, N_MIXERS))
    n_b = len(range(1, DEPTH, N_MIXERS))
    n_c = len(range(2, DEPTH, N_MIXERS))
    wa_cols = (WA_HEADS + 2 * WA_KV_HEADS) * WA_HEAD_DIM
    return {
        'x': normal((BATCH, SEQ, D_MODEL), 1.0),
        'c': normal((BATCH, D_MODEL), 1.0),
        'ctx': normal((BATCH, CTX_LEN, D_MODEL), 1.0),
        'c_ctx': normal((D_MODEL,), 1.0),
        'ada_a': normal((L, D_MODEL, ADA_RANK), D_MODEL ** -0.5),
        'ada_b': normal((L, ADA_RANK, N_MOD * D_MODEL), ADA_INIT * ADA_RANK ** -0.5),
        'ada_bias': normal((L, N_MOD * D_MODEL), 0.01),
        'norm_mix': 1.0 + normal((L, D_MODEL), 0.05),
        'norm_mlp': 1.0 + normal((L, D_MODEL), 0.05),
        'mlp_w1': normal((L, D_MODEL, MLP_HIDDEN), D_MODEL ** -0.5),
        'mlp_w2': normal((L, MLP_HIDDEN, D_MODEL), MLP_HIDDEN ** -0.5),
        'na_w_qkv': normal((n_a, D_MODEL, 3 * NA_HEADS * NA_HEAD_DIM), D_MODEL ** -0.5),
        'na_w_o': normal((n_a, NA_HEADS * NA_HEAD_DIM, D_MODEL), (NA_HEADS * NA_HEAD_DIM) ** -0.5),
        'na_q_norm': 1.0 + normal((n_a, NA_HEAD_DIM), 0.05),
        'na_k_norm': 1.0 + normal((n_a, NA_HEAD_DIM), 0.05),
        'na_rel_bias': normal((n_a, NA_HEADS, 2 * NA_KH - 1, 2 * NA_KW - 1), 0.2),
        'wa_w_qkv': normal((n_b, D_MODEL, wa_cols), D_MODEL ** -0.5),
        'wa_w_o': normal((n_b, WA_HEADS * WA_HEAD_DIM, D_MODEL), (WA_HEADS * WA_HEAD_DIM) ** -0.5),
        'wa_q_norm': 1.0 + normal((n_b, WA_HEAD_DIM), 0.05),
        'wa_k_norm': 1.0 + normal((n_b, WA_HEAD_DIM), 0.05),
        'wa_sink': normal((n_b, WA_HEADS), 1.0),
        'sg_w_in': normal((n_c, D_MODEL, 2 * SG_WIDTH), D_MODEL ** -0.5),
        'sg_v_norm': 1.0 + normal((n_c, SG_WIDTH), 0.05),
        'sg_w_s': normal((n_c, SG_GROUPS, SG_CHUNK, SG_CHUNK), SG_CHUNK ** -0.5),
        'sg_b_s': 1.0 + normal((n_c, SG_GROUPS, SG_CHUNK), 0.05),
        'sg_w_out': normal((n_c, SG_WIDTH, D_MODEL), SG_WIDTH ** -0.5),
    }


def reference(x, c, ctx, c_ctx, ada_a, ada_b, ada_bias, norm_mix, norm_mlp, mlp_w1, mlp_w2,
              na_w_qkv, na_w_o, na_q_norm, na_k_norm, na_rel_bias,
              wa_w_qkv, wa_w_o, wa_q_norm, wa_k_norm, wa_sink,
              sg_w_in, sg_v_norm, sg_w_s, sg_b_s, sg_w_out):
    cos, sin = axial_rope(x.shape[1], WA_HEAD_DIM)
    h_ctx = ctx
    for i in range(DEPTH):
        last = i == DEPTH - 1
        kind = i % N_MIXERS
        j = i // N_MIXERS
        sh1, sc1, gt1, sh2, sc2, gt2 = [m[:, None, :] for m in ada_modulation(c, ada_a[i], ada_b[i], ada_bias[i])]
        csh1, csc1, cgt1, csh2, csc2, cgt2 = ada_modulation(c_ctx, ada_a[i], ada_b[i], ada_bias[i])
        hx = modulate(rmsnorm(x, norm_mix[i]), sh1, sc1)
        hc = modulate(rmsnorm(h_ctx, norm_mix[i]), csh1, csc1)
        if kind == 0:
            ox, oc = neighborhood_attention(hx, hc, na_w_qkv[j], na_w_o[j], na_q_norm[j], na_k_norm[j],
                                            na_rel_bias[j], not last)
        elif kind == 1:
            ox, oc = window_attention(hx, hc, wa_w_qkv[j], wa_w_o[j], wa_q_norm[j], wa_k_norm[j],
                                      wa_sink[j], cos, sin, not last)
        else:
            ox, oc = spatial_gating_mlp(hx, hc, sg_w_in[j], sg_v_norm[j], sg_w_s[j], sg_b_s[j],
                                        sg_w_out[j], not last)
        x = x + gt1 * ox
        x = x + gt2 * sq_relu_mlp(modulate(rmsnorm(x, norm_mlp[i]), sh2, sc2), mlp_w1[i], mlp_w2[i])
        if not last:
            h_ctx = h_ctx + cgt1 * oc
            h_ctx = h_ctx + cgt2 * sq_relu_mlp(modulate(rmsnorm(h_ctx, norm_mlp[i]), csh2, csc2),
                                               mlp_w1[i], mlp_w2[i])
    return x
```

```python
import functools

import numpy as np
import jax
import jax.numpy as jnp
from jax import lax
from jax.experimental import pallas as pl
from jax.experimental.pallas import tpu as pltpu

GRID_W = 64
HEAD_DIM = 128
NA_KH = 8
NA_KW = 16
WA_KV_HEADS = 8
WA_WINDOW = 128
ROPE_BASE = 10000.0
SG_GROUPS = 8
SG_CHUNK = 128
N_MOD = 6
EPS = 1e-6
NEG_INF = -1e30

LANES = 128
V7X_VMEM_BYTES = 64 * 1024 * 1024
VMEM_CAP = V7X_VMEM_BYTES - 6 * 1024 * 1024

F32 = jnp.float32
BF16 = jnp.bfloat16
_NT = (((1,), (1,)), ((), ()))


def _vmem_limit(nbytes):
    return int(min(VMEM_CAP, nbytes + 8 * 1024 * 1024))


def _largest_divisor(n, candidates):
    for c in candidates:
        if n % c == 0:
            return c
    raise ValueError(f"no tile in {candidates} divides {n}")


def _mm_body(*refs, nk, bm, epilogue, prologue, n_x):
    x_ref, w_ref = refs[0], refs[1]
    pos = 2
    res_ref = vec_ref = None
    if epilogue == "gate_res":
        res_ref, vec_ref = refs[2], refs[3]
        pos = 4
    elif epilogue == "bias":
        vec_ref = refs[2]
        pos = 3
    o_ref, wbf_ref = refs[pos], refs[pos + 1]
    acc_ref = refs[pos + 2] if nk > 1 else None
    k = pl.program_id(1)
    m = pl.program_id(2)

    @pl.when(m == 0)
    def _cast_weights():
        wbf_ref[...] = w_ref[...].astype(BF16)

    xv = x_ref[...]
    if prologue == "silu":
        xv = xv * jax.nn.sigmoid(xv)
    part = jnp.dot(xv.astype(BF16), wbf_ref[...], preferred_element_type=F32)

    def finish(acc):
        if epilogue == "relu2":
            y = jnp.square(jnp.maximum(acc, 0.0))
        elif epilogue == "gelu":
            y = 0.5 * acc * (1.0 + lax.erf(acc * np.float32(np.sqrt(0.5))))
        elif epilogue == "gate_res":
            rows = m * bm + lax.broadcasted_iota(jnp.int32, (bm, 1), 0)
            gate = jnp.where(rows >= n_x, vec_ref[1:2, :], vec_ref[0:1, :])
            y = res_ref[...] + gate * acc
        elif epilogue == "bias":
            y = acc + vec_ref[...]
        else:
            y = acc
        o_ref[...] = y.astype(o_ref.dtype)

    if nk == 1:
        finish(part)
    else:
        rows = pl.ds(pl.multiple_of(m * bm, bm), bm)

        @pl.when(k == 0)
        def _init():
            acc_ref[rows, :] = part

        @pl.when(k > 0)
        def _accumulate():
            acc_ref[rows, :] += part

        @pl.when(k == nk - 1)
        def _emit():
            finish(acc_ref[rows, :])


def _matmul(x, w, *, out_dtype, epilogue=None, prologue=None, res=None, vec=None, vec_col=0,
            n_x=0, m_rows=None, bm=None, bn=None, bk=None, name="mm"):
    M = x.shape[0] if m_rows is None else m_rows
    K, N = w.shape
    bm = bm or _largest_divisor(M, (1024, 768, 512, 384, 256, 128, 16))
    bn = bn or _largest_divisor(N, (512, 256, 128))
    bk = bk or (K if K <= 4096 else _largest_divisor(K, (2048, 1024, 512, 256, 128)))
    nk = K // bk
    last_k = nk - 1

    def out_rows(k, m):
        return m if nk == 1 else jnp.where(k == last_k, m, 0)

    in_specs = [
        pl.BlockSpec((bm, bk), lambda n, k, m: (m, k)),
        pl.BlockSpec((bk, bn), lambda n, k, m: (k, n)),
    ]
    args = [x, w]
    nbytes = 2 * bm * bk * x.dtype.itemsize + 2 * bk * bn * 4 + bk * bn * 2
    if epilogue == "gate_res":
        in_specs.append(pl.BlockSpec((bm, bn), lambda n, k, m: (out_rows(k, m), n)))
        in_specs.append(pl.BlockSpec((vec.shape[0], bn), lambda n, k, m: (0, vec_col * (N // bn) + n)))
        args += [res, vec]
        nbytes += 2 * bm * bn * 4 + 2 * vec.shape[0] * bn * 4
    elif epilogue == "bias":
        in_specs.append(pl.BlockSpec((1, bn), lambda n, k, m: (0, n)))
        args.append(vec)
        nbytes += 2 * 8 * bn * 4
    scratch = [pltpu.VMEM((bk, bn), BF16)]
    if nk > 1:
        scratch.append(pltpu.VMEM((M, bn), F32))
        nbytes += M * bn * 4
    nbytes += 2 * bm * bn * jnp.dtype(out_dtype).itemsize
    body = functools.partial(_mm_body, nk=nk, bm=bm, epilogue=epilogue, prologue=prologue, n_x=n_x)
    return pl.pallas_call(
        body,
        grid=(N // bn, nk, M // bm),
        in_specs=in_specs,
        out_specs=pl.BlockSpec((bm, bn), lambda n, k, m: (out_rows(k, m), n)),
        out_shape=jax.ShapeDtypeStruct((M, N), out_dtype),
        scratch_shapes=scratch,
        compiler_params=pltpu.CompilerParams(
            dimension_semantics=("arbitrary", "arbitrary", "arbitrary"),
            vmem_limit_bytes=_vmem_limit(nbytes)),
        name=name,
    )(*args)


def _norm_body(x_ref, g_ref, sh_ref, sc_ref, o_ref, *, nx_tiles):
    t = (pl.program_id(0) >= nx_tiles).astype(jnp.int32)
    x = x_ref[...]
    y = x * lax.rsqrt(jnp.mean(x * x, axis=-1, keepdims=True) + EPS)
    y = y * g_ref[...]
    shift = sh_ref[pl.ds(t, 1), :]
    scale = sc_ref[pl.ds(t, 1), :]
    o_ref[...] = (y * (1.0 + scale) + shift).astype(o_ref.dtype)


def _norm_modulate(x, gain, mods, *, shift_col, scale_col, n_x, m_rows=None):
    M = x.shape[0] if m_rows is None else m_rows
    D = x.shape[1]
    n_ctx = x.shape[0] - n_x
    bm = _largest_divisor(int(np.gcd(n_x, n_ctx)), (256, 128, 64, 32, 16))
    nbytes = 2 * bm * D * 4 + 2 * bm * D * 2 + 6 * 8 * D * 4
    return pl.pallas_call(
        functools.partial(_norm_body, nx_tiles=n_x // bm),
        grid=(M // bm,),
        in_specs=[
            pl.BlockSpec((bm, D), lambda i: (i, 0)),
            pl.BlockSpec((1, D), lambda i: (0, 0)),
            pl.BlockSpec((mods.shape[0], D), lambda i: (0, shift_col)),
            pl.BlockSpec((mods.shape[0], D), lambda i: (0, scale_col)),
        ],
        out_specs=pl.BlockSpec((bm, D), lambda i: (i, 0)),
        out_shape=jax.ShapeDtypeStruct((M, D), BF16),
        compiler_params=pltpu.CompilerParams(
            dimension_semantics=("arbitrary",), vmem_limit_bytes=_vmem_limit(nbytes)),
        name="norm_modulate",
    )(x, gain.reshape(1, D), mods, mods)


def _rms_gain(xf, gain):
    return (xf * lax.rsqrt(jnp.mean(xf * xf, axis=-1, keepdims=True) + EPS)) * gain


def _softmax_pv(parts, extra_logit=None):
    m = None
    for s, _ in parts:
        ms = jnp.max(s, axis=-1, keepdims=True)
        m = ms if m is None else jnp.maximum(m, ms)
    if extra_logit is not None:
        m = jnp.maximum(m, extra_logit)
    l = None
    o = None
    for s, v in parts:
        p = jnp.exp(s - m)
        ls = jnp.sum(p, axis=-1, keepdims=True)
        l = ls if l is None else l + ls
        ov = jnp.dot(p.astype(BF16), v, preferred_element_type=F32)
        o = ov if o is None else o + ov
    if extra_logit is not None:
        l = l + jnp.exp(extra_logit - m)
    return o / l


def _na_body(q_ref, k_ref, v_ref, qg_ref, kg_ref, tb_ref, o_ref, qn_ref, kn_ref, *,
             n_x, n_ctx, rows, kh):
    W = GRID_W
    scale = np.float32(HEAD_DIM ** -0.5)
    nkr = kh + 2
    chunk = 128

    def norm_chunk(c, carry):
        r = pl.ds(pl.multiple_of(c * chunk, chunk), chunk)
        qn_ref[r, :] = (_rms_gain(q_ref[r, :].astype(F32), qg_ref[...]) * scale).astype(BF16)
        kn_ref[r, :] = _rms_gain(k_ref[r, :].astype(F32), kg_ref[...]).astype(BF16)
        return carry

    lax.fori_loop(0, (n_x + n_ctx) // chunk, norm_chunk, 0)

    kc = kn_ref[n_x:n_x + n_ctx, :]
    vc = v_ref[n_x:n_x + n_ctx, :]
    half = lax.broadcasted_iota(jnp.int32, (W, 2 * W), 1) // W

    def step(i, carry):
        r = 2 * i
        a = jnp.clip(r - kh // 2, 0, rows - nkr)
        qrows = pl.ds(pl.multiple_of(r * W, 2 * W), 2 * W)
        krows = pl.ds(pl.multiple_of(a * W, 2 * W), nkr * W)
        q2 = qn_ref[qrows, :]
        s_loc = lax.dot_general(q2, kn_ref[krows, :], _NT, preferred_element_type=F32)
        bias_rows = []
        for j in range(2):
            qr = r + j
            r0 = jnp.clip(qr - kh // 2, 0, rows - kh)
            pieces = []
            for t in range(nkr // 2):
                kr0 = a + 2 * t
                plane = jnp.clip(kr0 - qr + kh, 0, 2 * kh - 1)
                kr = kr0 + half
                ok = (kr >= r0) & (kr < r0 + kh)
                pieces.append(jnp.where(ok, tb_ref[0, plane], NEG_INF))
            bias_rows.append(jnp.concatenate(pieces, axis=1))
        s_loc = s_loc + jnp.concatenate(bias_rows, axis=0)
        s_ctx = lax.dot_general(q2, kc, _NT, preferred_element_type=F32)
        o = _softmax_pv([(s_loc, v_ref[krows, :]), (s_ctx, vc)])
        o_ref[qrows, :] = o.astype(o_ref.dtype)
        return carry

    lax.fori_loop(0, rows // 2, step, 0)

    qc = qn_ref[n_x:n_x + n_ctx, :]
    s = lax.dot_general(qc, kc, _NT, preferred_element_type=F32)
    o_ref[n_x:n_x + n_ctx, :] = _softmax_pv([(s, vc)]).astype(o_ref.dtype)


def _na_bias_planes(rel_bias):
    W, kw = GRID_W, NA_KW
    qc = np.arange(W)[:, None]
    kc = np.arange(W)[None, :]
    win = np.clip(qc - kw // 2, 0, W - kw)
    ok = (kc >= win) & (kc < win + kw)
    dc = np.clip(kc - qc, -(kw - 1), kw - 1) + kw - 1
    t = jnp.where(jnp.asarray(ok), rel_bias[:, :, dc].astype(F32), NEG_INF)
    neg = jnp.full(t.shape[:1] + (1,) + t.shape[2:], NEG_INF, F32)
    text = jnp.concatenate([neg, t, neg], axis=1)
    return jnp.concatenate([text[:, :-1], text[:, 1:]], axis=-1)


def _na_attention(qkv, q_gain, k_gain, rel_bias, *, n_x, n_ctx):
    mtot, width = qkv.shape
    H = width // (3 * HEAD_DIM)
    rows = n_x // GRID_W
    kh = min(NA_KH, rows)
    assert GRID_W * 2 == LANES and rows % 2 == 0 and rows >= kh + 2 and kh % 2 == 0
    planes = _na_bias_planes(rel_bias)
    blk = (mtot, HEAD_DIM)
    nbytes = 8 * mtot * HEAD_DIM * 2 + 2 * mtot * HEAD_DIM * 2 + 2 * planes[0].size * 4
    return pl.pallas_call(
        functools.partial(_na_body, n_x=n_x, n_ctx=n_ctx, rows=rows, kh=kh),
        grid=(H,),
        in_specs=[
            pl.BlockSpec(blk, lambda h: (0, h)),
            pl.BlockSpec(blk, lambda h: (0, H + h)),
            pl.BlockSpec(blk, lambda h: (0, 2 * H + h)),
            pl.BlockSpec((1, HEAD_DIM), lambda h: (0, 0)),
            pl.BlockSpec((1, HEAD_DIM), lambda h: (0, 0)),
            pl.BlockSpec((1,) + planes.shape[1:], lambda h: (h, 0, 0, 0)),
        ],
        out_specs=pl.BlockSpec(blk, lambda h: (0, h)),
        out_shape=jax.ShapeDtypeStruct((mtot, H * HEAD_DIM), BF16),
        scratch_shapes=[pltpu.VMEM(blk, BF16), pltpu.VMEM(blk, BF16)],
        compiler_params=pltpu.CompilerParams(
            dimension_semantics=("arbitrary",), vmem_limit_bytes=_vmem_limit(nbytes)),
        name="na_attention",
    )(qkv, qkv, qkv, q_gain.reshape(1, HEAD_DIM), k_gain.reshape(1, HEAD_DIM), planes)


def _wa_body(q_ref, k_ref, v_ref, qg_ref, kg_ref, ca_ref, sa_ref, cb_ref, sb_ref, sink_ref,
             o_ref, kn_ref, *, n_x, n_ctx, group):
    W = WA_WINDOW
    rpb = W // GRID_W
    nblk = n_x // W
    scale = np.float32(HEAD_DIM ** -0.5)
    g = pl.program_id(1)
    lane = lax.broadcasted_iota(jnp.int32, (W, HEAD_DIM), 1)
    neg_half = (lane % (HEAD_DIM // 2)) < HEAD_DIM // 4

    def rope(y, b):
        def table(row_ref, col_ref):
            row_part = [jnp.broadcast_to(row_ref[pl.ds(rpb * b + j, 1), :], (GRID_W, HEAD_DIM))
                        for j in range(rpb)]
            return jnp.concatenate(row_part, axis=0) + jnp.concatenate([col_ref[...]] * rpb, axis=0)
        rot = jnp.where(neg_half, -pltpu.roll(y, HEAD_DIM - HEAD_DIM // 4, 1),
                        pltpu.roll(y, HEAD_DIM // 4, 1))
        return y * table(ca_ref, cb_ref) + rot * table(sa_ref, sb_ref)

    @pl.when(g == 0)
    def _prepare_keys():
        def kblock(b, carry):
            r = pl.ds(pl.multiple_of(b * W, W), W)
            kn_ref[r, :] = rope(_rms_gain(k_ref[r, :].astype(F32), kg_ref[...]), b).astype(BF16)
            return carry
        lax.fori_loop(0, nblk, kblock, 0)
        for c in range(n_ctx // W):
            r = slice(n_x + c * W, n_x + (c + 1) * W)
            kn_ref[r, :] = _rms_gain(k_ref[r, :].astype(F32), kg_ref[...]).astype(BF16)

    sink = sink_ref[pl.program_id(0) * group + g]
    kc = kn_ref[n_x:n_x + n_ctx, :]
    vc = v_ref[n_x:n_x + n_ctx, :]
    j_minus_i = (lax.broadcasted_iota(jnp.int32, (W, 3 * W), 1)
                 - lax.broadcasted_iota(jnp.int32, (W, 3 * W), 0))

    def qblock(b, carry):
        r = pl.ds(pl.multiple_of(b * W, W), W)
        qn = (rope(_rms_gain(q_ref[r, :].astype(F32), qg_ref[...]), b) * scale).astype(BF16)
        a = jnp.clip((b - 1) * W, 0, n_x - 3 * W)
        krows = pl.ds(pl.multiple_of(a, W), 3 * W)
        s_loc = lax.dot_general(qn, kn_ref[krows, :], _NT, preferred_element_type=F32)
        dist = j_minus_i + (a - b * W)
        s_loc = jnp.where((dist >= -W) & (dist <= W), s_loc, NEG_INF)
        s_ctx = lax.dot_general(qn, kc, _NT, preferred_element_type=F32)
        o = _softmax_pv([(s_loc, v_ref[krows, :]), (s_ctx, vc)], extra_logit=sink)
        o_ref[r, :] = o.astype(o_ref.dtype)
        return carry

    lax.fori_loop(0, nblk, qblock, 0)

    for c in range(n_ctx // W):
        r = slice(n_x + c * W, n_x + (c + 1) * W)
        qc = (_rms_gain(q_ref[r, :].astype(F32), qg_ref[...]) * scale).astype(BF16)
        s = lax.dot_general(qc, kc, _NT, preferred_element_type=F32)
        o_ref[r, :] = _softmax_pv([(s, vc)], extra_logit=sink).astype(o_ref.dtype)


def _rope_tables(rows):
    axis_dim = HEAD_DIM // 2
    inv = ROPE_BASE ** (-jnp.arange(0, axis_dim, 2, dtype=F32) / axis_dim)
    ar = jnp.arange(rows, dtype=F32)[:, None] * inv
    ac = jnp.arange(GRID_W, dtype=F32)[:, None] * inv
    zr = jnp.zeros((rows, axis_dim), F32)
    zc = jnp.zeros((GRID_W, axis_dim), F32)

    def row_tab(f):
        return jnp.concatenate([f(ar), f(ar), zr], axis=-1)

    def col_tab(f):
        return jnp.concatenate([zc, f(ac), f(ac)], axis=-1)

    return row_tab(jnp.cos), row_tab(jnp.sin), col_tab(jnp.cos), col_tab(jnp.sin)


def _wa_attention(qkv, q_gain, k_gain, sink, *, n_x, n_ctx, n_heads):
    mtot = qkv.shape[0]
    H, KVH = n_heads, WA_KV_HEADS
    G = H // KVH
    W = WA_WINDOW
    assert W % GRID_W == 0 and n_x % W == 0 and n_x >= 3 * W and n_ctx % W == 0 and W == HEAD_DIM
    ca, sa, cb, sb = _rope_tables(n_x // GRID_W)
    blk = (mtot, HEAD_DIM)
    full = lambda arr: pl.BlockSpec(arr.shape, lambda kv, g: (0,) * arr.ndim)
    nbytes = 8 * mtot * HEAD_DIM * 2 + mtot * HEAD_DIM * 2 + 4 * (ca.size + cb.size) * 4
    return pl.pallas_call(
        functools.partial(_wa_body, n_x=n_x, n_ctx=n_ctx, group=G),
        grid=(KVH, G),
        in_specs=[
            pl.BlockSpec(blk, lambda kv, g: (0, kv * G + g)),
            pl.BlockSpec(blk, lambda kv, g: (0, H + kv)),
            pl.BlockSpec(blk, lambda kv, g: (0, H + KVH + kv)),
            pl.BlockSpec((1, HEAD_DIM), lambda kv, g: (0, 0)),
            pl.BlockSpec((1, HEAD_DIM), lambda kv, g: (0, 0)),
            full(ca), full(sa), full(cb), full(sb),
            pl.BlockSpec(memory_space=pltpu.SMEM),
        ],
        out_specs=pl.BlockSpec(blk, lambda kv, g: (0, kv * G + g)),
        out_shape=jax.ShapeDtypeStruct((mtot, H * HEAD_DIM), BF16),
        scratch_shapes=[pltpu.VMEM(blk, BF16)],
        compiler_params=pltpu.CompilerParams(
            dimension_semantics=("arbitrary", "arbitrary"), vmem_limit_bytes=_vmem_limit(nbytes)),
        name="wa_attention",
    )(qkv, qkv, qkv, q_gain.reshape(1, HEAD_DIM), k_gain.reshape(1, HEAD_DIM), ca, sa, cb, sb,
      sink.astype(F32))


def _sg_body(u_ref, v_ref, vg_ref, ws_ref, bs_ref, o_ref, *, n_chunks, groups):
    P = SG_CHUNK
    gw = v_ref.shape[1] // groups
    vn = _rms_gain(v_ref[...].astype(F32), vg_ref[...]).astype(BF16)
    for g in range(groups):
        wg = ws_ref[g].astype(BF16)
        cols = slice(g * gw, (g + 1) * gw)
        for c in range(n_chunks):
            rows = slice(c * P, (c + 1) * P)
            sv = jnp.dot(wg, vn[rows, cols], preferred_element_type=F32) + bs_ref[g]
            o_ref[rows, cols] = (u_ref[rows, cols].astype(F32) * sv).astype(o_ref.dtype)


def _spatial_gate(uv, v_gain, w_s, b_s):
    mtot, two_d = uv.shape
    D = two_d // 2
    G, P = SG_GROUPS, SG_CHUNK
    gw = D // G
    assert gw % LANES == 0 and mtot % P == 0
    n_chunks = _largest_divisor(mtot // P, (3, 2, 1))
    bm = n_chunks * P
    bias = jnp.broadcast_to(b_s.astype(F32)[:, :, None], (G, P, gw))
    nbytes = 6 * bm * D * 2 + bm * D * 6 + 2 * (w_s.size + bias.size) * 4
    return pl.pallas_call(
        functools.partial(_sg_body, n_chunks=n_chunks, groups=G),
        grid=(mtot // bm,),
        in_specs=[
            pl.BlockSpec((bm, D), lambda i: (i, 0)),
            pl.BlockSpec((bm, D), lambda i: (i, 1)),
            pl.BlockSpec((1, D), lambda i: (0, 0)),
            pl.BlockSpec(w_s.shape, lambda i: (0, 0, 0)),
            pl.BlockSpec(bias.shape, lambda i: (0, 0, 0)),
        ],
        out_specs=pl.BlockSpec((bm, D), lambda i: (i, 0)),
        out_shape=jax.ShapeDtypeStruct((mtot, D), BF16),
        compiler_params=pltpu.CompilerParams(
            dimension_semantics=("arbitrary",), vmem_limit_bytes=_vmem_limit(nbytes)),
        name="spatial_gate",
    )(uv, uv, v_gain.reshape(1, D), w_s, bias)


def _ada_modulation(cond, a, b, bias):
    t = _matmul(cond, a, out_dtype=F32, prologue="silu", name="ada_down")
    return _matmul(t, b, out_dtype=F32, epilogue="bias", vec=bias.reshape(1, -1),
                   bn=_largest_divisor(b.shape[1], (2048, 1024, 512, 256, 128)), name="ada_up")


def kernel(x, c, ctx, c_ctx, ada_a, ada_b, ada_bias, norm_mix, norm_mlp, mlp_w1, mlp_w2,
           na_w_qkv, na_w_o, na_q_norm, na_k_norm, na_rel_bias,
           wa_w_qkv, wa_w_o, wa_q_norm, wa_k_norm, wa_sink,
           sg_w_in, sg_v_norm, sg_w_s, sg_b_s, sg_w_out):
    assert x.shape[0] == 1 and ctx.shape[0] == 1 and c.shape[0] == 1
    n_x, D = x.shape[1], x.shape[2]
    n_ctx = ctx.shape[1]
    depth = ada_a.shape[0]
    n_heads = D // HEAD_DIM
    n_mixers = 3

    xc = jnp.concatenate([x[0], ctx[0]], axis=0)
    cond = jnp.zeros((16, D), F32).at[0].set(c[0]).at[1].set(c_ctx)

    for i in range(depth):
        last = i == depth - 1
        kind, j = i % n_mixers, i // n_mixers
        mods = _ada_modulation(cond, ada_a[i], ada_b[i], ada_bias[i])
        gate = dict(epilogue="gate_res", vec=mods, n_x=n_x, out_dtype=F32,
                    m_rows=n_x if last else None)

        h = _norm_modulate(xc, norm_mix[i], mods, shift_col=0, scale_col=1, n_x=n_x)
        if kind == 0:
            qkv = _matmul(h, na_w_qkv[j], out_dtype=BF16, name="na_qkv")
            o = _na_attention(qkv, na_q_norm[j], na_k_norm[j], na_rel_bias[j], n_x=n_x, n_ctx=n_ctx)
            xc = _matmul(o, na_w_o[j], res=xc, vec_col=2, name="na_out", **gate)
        elif kind == 1:
            qkv = _matmul(h, wa_w_qkv[j], out_dtype=BF16, name="wa_qkv")
            o = _wa_attention(qkv, wa_q_norm[j], wa_k_norm[j], wa_sink[j], n_x=n_x, n_ctx=n_ctx,
                              n_heads=n_heads)
            xc = _matmul(o, wa_w_o[j], res=xc, vec_col=2, name="wa_out", **gate)
        else:
            uv = _matmul(h, sg_w_in[j], out_dtype=BF16, epilogue="gelu", name="sg_in")
            z = _spatial_gate(uv, sg_v_norm[j], sg_w_s[j], sg_b_s[j])
            xc = _matmul(z, sg_w_out[j], res=xc, vec_col=2, name="sg_out", **gate)

        h2 = _norm_modulate(xc, norm_mlp[i], mods, shift_col=3, scale_col=4, n_x=n_x)
        hid = _matmul(h2, mlp_w1[i], out_dtype=BF16, epilogue="relu2", name="mlp_up")
        xc = _matmul(hid, mlp_w2[i], res=xc, vec_col=5, name="mlp_down", **gate)
    return xc[None]
```

```python
import functools

import numpy as np
import jax
import jax.numpy as jnp
from jax import lax
from jax.experimental import pallas as pl
from jax.experimental.pallas import tpu as pltpu

GRID_W = 64
HEAD_DIM = 128
NA_KH = 8
NA_KW = 16
WA_KV_HEADS = 8
WA_WINDOW = 128
ROPE_BASE = 10000.0
SG_GROUPS = 8
SG_CHUNK = 128
N_MOD = 6
EPS = 1e-6
NEG_INF = -1e30

LANES = 128
V7X_VMEM_BYTES = 64 * 1024 * 1024
VMEM_CAP = V7X_VMEM_BYTES - 6 * 1024 * 1024
VMEM_SLACK = 2 * 1024 * 1024

F32 = jnp.float32
BF16 = jnp.bfloat16
_NT = (((1,), (1,)), ((), ()))


def _vmem_limit(nbytes):
    return int(min(VMEM_CAP, nbytes + 4 * VMEM_SLACK))


def _largest_divisor(n, candidates):
    for c in candidates:
        if n % c == 0:
            return c
    raise ValueError(f"no tile in {candidates} divides {n}")


def _mm_body(*refs, nk, bm, epilogue, prologue, n_x):
    x_ref, w_ref = refs[0], refs[1]
    pos = 2
    res_ref = vec_ref = None
    if epilogue == "gate_res":
        res_ref, vec_ref = refs[2], refs[3]
        pos = 4
    elif epilogue == "bias":
        vec_ref = refs[2]
        pos = 3
    o_ref, wbf_ref = refs[pos], refs[pos + 1]
    acc_ref = refs[pos + 2] if nk > 1 else None
    k = pl.program_id(1)
    m = pl.program_id(2)

    @pl.when(m == 0)
    def _cast_weights():
        wbf_ref[...] = w_ref[...].astype(BF16)

    if nk > 1:
        rows = pl.ds(pl.multiple_of(m * bm, bm), bm)

        @pl.when(k == 0)
        def _zero():
            acc_ref[rows, :] = jnp.zeros((bm, acc_ref.shape[1]), F32)

    xv = x_ref[...]
    if prologue == "silu":
        xv = xv * jax.nn.sigmoid(xv)
    part = jnp.dot(xv.astype(BF16), wbf_ref[...], preferred_element_type=F32)

    def finish(acc):
        if epilogue == "relu2":
            y = jnp.square(jnp.maximum(acc, 0.0))
        elif epilogue == "gelu":
            y = 0.5 * acc * (1.0 + lax.erf(acc * np.float32(np.sqrt(0.5))))
        elif epilogue == "gate_res":
            rows = m * bm + lax.broadcasted_iota(jnp.int32, (bm, 1), 0)
            gate = jnp.where(rows >= n_x, vec_ref[1:2, :], vec_ref[0:1, :])
            y = res_ref[...] + gate * acc
        elif epilogue == "bias":
            y = acc + vec_ref[...]
        else:
            y = acc
        o_ref[...] = y.astype(o_ref.dtype)

    if nk == 1:
        finish(part)
    else:
        acc_ref[rows, :] = acc_ref[rows, :] + part

        @pl.when(k == nk - 1)
        def _emit():
            finish(acc_ref[rows, :])


def _matmul(x, w, layer, *, out_dtype, epilogue=None, prologue=None, res=None, vec=None, vec_col=0,
            n_x=0, m_rows=None, bm=None, bn=None, bk=None, name="mm"):
    M = x.shape[0] if m_rows is None else m_rows
    _, K, N = w.shape
    bn = bn or _largest_divisor(N, (512, 256, 128))
    bk = bk or _largest_divisor(K, (4096, 2048, 1024, 512, 256, 128))
    nk = K // bk
    last_k = nk - 1
    out_bytes = jnp.dtype(out_dtype).itemsize

    def vmem_bytes(bm_):
        n = 2 * bm_ * bk * x.dtype.itemsize + 2 * bk * bn * 4 + bk * bn * 2 + 2 * bm_ * bn * out_bytes
        if epilogue == "gate_res":
            n += 2 * bm_ * bn * 4 + 2 * vec.shape[0] * bn * 4
        if nk > 1:
            n += M * bn * 4
        return n

    if bm is None:
        bm = next(c for c in (1024, 768, 512, 384, 256, 128, 16)
                  if M % c == 0 and vmem_bytes(c) <= VMEM_CAP - VMEM_SLACK)

    def out_rows(k, m):
        return m if nk == 1 else jnp.where(k == last_k, m, 0)

    in_specs = [
        pl.BlockSpec((bm, bk), lambda n, k, m: (m, k)),
        pl.BlockSpec((None, bk, bn), lambda n, k, m: (layer, k, n)),
    ]
    args = [x, w]
    if epilogue == "gate_res":
        in_specs.append(pl.BlockSpec((bm, bn), lambda n, k, m: (out_rows(k, m), n)))
        in_specs.append(pl.BlockSpec((vec.shape[0], bn), lambda n, k, m: (0, vec_col * (N // bn) + n)))
        args += [res, vec]
    elif epilogue == "bias":
        in_specs.append(pl.BlockSpec((None, 1, bn), lambda n, k, m: (layer, 0, n)))
        args.append(vec)
    scratch = [pltpu.VMEM((bk, bn), BF16)]
    if nk > 1:
        scratch.append(pltpu.VMEM((M, bn), F32))
    body = functools.partial(_mm_body, nk=nk, bm=bm, epilogue=epilogue, prologue=prologue, n_x=n_x)
    return pl.pallas_call(
        body,
        grid=(N // bn, nk, M // bm),
        in_specs=in_specs,
        out_specs=pl.BlockSpec((bm, bn), lambda n, k, m: (out_rows(k, m), n)),
        out_shape=jax.ShapeDtypeStruct((M, N), out_dtype),
        scratch_shapes=scratch,
        compiler_params=pltpu.CompilerParams(
            dimension_semantics=("arbitrary", "arbitrary", "arbitrary"),
            vmem_limit_bytes=_vmem_limit(vmem_bytes(bm))),
        name=name,
    )(*args)


def _norm_body(x_ref, g_ref, sh_ref, sc_ref, o_ref, *, nx_tiles):
    t = (pl.program_id(0) >= nx_tiles).astype(jnp.int32)
    x = x_ref[...]
    y = x * lax.rsqrt(jnp.mean(x * x, axis=-1, keepdims=True) + EPS)
    y = y * g_ref[...]
    shift = sh_ref[pl.ds(t, 1), :]
    scale = sc_ref[pl.ds(t, 1), :]
    o_ref[...] = (y * (1.0 + scale) + shift).astype(o_ref.dtype)


def _norm_modulate(x, gain, mods, *, shift_col, scale_col, n_x, m_rows=None):
    M = x.shape[0] if m_rows is None else m_rows
    D = x.shape[1]
    n_ctx = x.shape[0] - n_x
    bm = _largest_divisor(int(np.gcd(n_x, n_ctx)), (256, 128, 64, 32, 16))
    nbytes = 2 * bm * D * 4 + 2 * bm * D * 2 + 6 * 8 * D * 4
    return pl.pallas_call(
        functools.partial(_norm_body, nx_tiles=n_x // bm),
        grid=(M // bm,),
        in_specs=[
            pl.BlockSpec((bm, D), lambda i: (i, 0)),
            pl.BlockSpec((1, D), lambda i: (0, 0)),
            pl.BlockSpec((mods.shape[0], D), lambda i: (0, shift_col)),
            pl.BlockSpec((mods.shape[0], D), lambda i: (0, scale_col)),
        ],
        out_specs=pl.BlockSpec((bm, D), lambda i: (i, 0)),
        out_shape=jax.ShapeDtypeStruct((M, D), BF16),
        compiler_params=pltpu.CompilerParams(
            dimension_semantics=("arbitrary",), vmem_limit_bytes=_vmem_limit(nbytes)),
        name="norm_modulate",
    )(x, gain.reshape(1, D), mods, mods)


def _rms_gain(xf, gain):
    return (xf * lax.rsqrt(jnp.mean(xf * xf, axis=-1, keepdims=True) + EPS)) * gain


def _softmax_pv(parts, extra_logit=None):
    m = None
    for s, _ in parts:
        ms = jnp.max(s, axis=-1, keepdims=True)
        m = ms if m is None else jnp.maximum(m, ms)
    if extra_logit is not None:
        m = jnp.maximum(m, extra_logit)
    l = None
    o = None
    for s, v in parts:
        p = jnp.exp(s - m)
        ls = jnp.sum(p, axis=-1, keepdims=True)
        l = ls if l is None else l + ls
        ov = jnp.dot(p.astype(BF16), v, preferred_element_type=F32)
        o = ov if o is None else o + ov
    if extra_logit is not None:
        l = l + jnp.exp(extra_logit - m)
    return o / l


def _na_body(q_ref, k_ref, v_ref, qg_ref, kg_ref, tb_ref, o_ref, qn_ref, kn_ref, *,
             n_x, n_ctx, rows, kh):
    W = GRID_W
    scale = np.float32(HEAD_DIM ** -0.5)
    nkr = kh + 2
    chunk = 128

    def norm_chunk(c, carry):
        r = pl.ds(pl.multiple_of(c * chunk, chunk), chunk)
        qn_ref[r, :] = (_rms_gain(q_ref[r, :].astype(F32), qg_ref[...]) * scale).astype(BF16)
        kn_ref[r, :] = _rms_gain(k_ref[r, :].astype(F32), kg_ref[...]).astype(BF16)
        return carry

    lax.fori_loop(0, (n_x + n_ctx) // chunk, norm_chunk, 0, unroll=2)

    kc = kn_ref[n_x:n_x + n_ctx, :]
    vc = v_ref[n_x:n_x + n_ctx, :]
    half = lax.broadcasted_iota(jnp.int32, (W, 2 * W), 1) // W

    def step(i, carry):
        r = 2 * i
        a = jnp.clip(r - kh // 2, 0, rows - nkr)
        qrows = pl.ds(pl.multiple_of(r * W, 2 * W), 2 * W)
        krows = pl.ds(pl.multiple_of(a * W, 2 * W), nkr * W)
        q2 = qn_ref[qrows, :]
        s_loc = lax.dot_general(q2, kn_ref[krows, :], _NT, preferred_element_type=F32)
        bias_rows = []
        for j in range(2):
            qr = r + j
            r0 = jnp.clip(qr - kh // 2, 0, rows - kh)
            pieces = []
            for t in range(nkr // 2):
                kr0 = a + 2 * t
                plane = jnp.clip(kr0 - qr + kh, 0, 2 * kh - 1)
                kr = kr0 + half
                ok = (kr >= r0) & (kr < r0 + kh)
                pieces.append(jnp.where(ok, tb_ref[0, plane], NEG_INF))
            bias_rows.append(jnp.concatenate(pieces, axis=1))
        s_loc = s_loc + jnp.concatenate(bias_rows, axis=0)
        s_ctx = lax.dot_general(q2, kc, _NT, preferred_element_type=F32)
        o = _softmax_pv([(s_loc, v_ref[krows, :]), (s_ctx, vc)])
        o_ref[qrows, :] = o.astype(o_ref.dtype)
        return carry

    lax.fori_loop(0, rows // 2, step, 0, unroll=2)

    qc = qn_ref[n_x:n_x + n_ctx, :]
    s = lax.dot_general(qc, kc, _NT, preferred_element_type=F32)
    o_ref[n_x:n_x + n_ctx, :] = _softmax_pv([(s, vc)]).astype(o_ref.dtype)


def _na_bias_planes(rel_bias):
    W, kw = GRID_W, NA_KW
    qc = np.arange(W)[:, None]
    kc = np.arange(W)[None, :]
    win = np.clip(qc - kw // 2, 0, W - kw)
    ok = (kc >= win) & (kc < win + kw)
    dc = np.clip(kc - qc, -(kw - 1), kw - 1) + kw - 1
    t = jnp.where(jnp.asarray(ok), rel_bias[:, :, dc].astype(F32), NEG_INF)
    neg = jnp.full(t.shape[:1] + (1,) + t.shape[2:], NEG_INF, F32)
    text = jnp.concatenate([neg, t, neg], axis=1)
    return jnp.concatenate([text[:, :-1], text[:, 1:]], axis=-1)


def _na_attention(qkv, q_gain, k_gain, rel_bias, *, n_x, n_ctx):
    mtot, width = qkv.shape
    H = width // (3 * HEAD_DIM)
    rows = n_x // GRID_W
    kh = min(NA_KH, rows)
    assert GRID_W * 2 == LANES and rows % 2 == 0 and rows >= kh + 2 and kh % 2 == 0
    planes = _na_bias_planes(rel_bias)
    blk = (mtot, HEAD_DIM)
    nbytes = 8 * mtot * HEAD_DIM * 2 + 2 * mtot * HEAD_DIM * 2 + 2 * planes[0].size * 4
    return pl.pallas_call(
        functools.partial(_na_body, n_x=n_x, n_ctx=n_ctx, rows=rows, kh=kh),
        grid=(H,),
        in_specs=[
            pl.BlockSpec(blk, lambda h: (0, h)),
            pl.BlockSpec(blk, lambda h: (0, H + h)),
            pl.BlockSpec(blk, lambda h: (0, 2 * H + h)),
            pl.BlockSpec((1, HEAD_DIM), lambda h: (0, 0)),
            pl.BlockSpec((1, HEAD_DIM), lambda h: (0, 0)),
            pl.BlockSpec((1,) + planes.shape[1:], lambda h: (h, 0, 0, 0)),
        ],
        out_specs=pl.BlockSpec(blk, lambda h: (0, h)),
        out_shape=jax.ShapeDtypeStruct((mtot, H * HEAD_DIM), BF16),
        scratch_shapes=[pltpu.VMEM(blk, BF16), pltpu.VMEM(blk, BF16)],
        compiler_params=pltpu.CompilerParams(
            dimension_semantics=("arbitrary",), vmem_limit_bytes=_vmem_limit(nbytes)),
        name="na_attention",
    )(qkv, qkv, qkv, q_gain.reshape(1, HEAD_DIM), k_gain.reshape(1, HEAD_DIM), planes)


def _wa_body(q_ref, k_ref, v_ref, qg_ref, kg_ref, ca_ref, sa_ref, cb_ref, sb_ref, sink_ref,
             o_ref, kn_ref, *, n_x, n_ctx, group):
    W = WA_WINDOW
    rpb = W // GRID_W
    nblk = n_x // W
    scale = np.float32(HEAD_DIM ** -0.5)
    g = pl.program_id(1)
    lane = lax.broadcasted_iota(jnp.int32, (W, HEAD_DIM), 1)
    neg_half = (lane % (HEAD_DIM // 2)) < HEAD_DIM // 4

    def rope(y, b):
        def table(row_ref, col_ref):
            row_part = [jnp.broadcast_to(row_ref[pl.ds(rpb * b + j, 1), :], (GRID_W, HEAD_DIM))
                        for j in range(rpb)]
            return jnp.concatenate(row_part, axis=0) + jnp.concatenate([col_ref[...]] * rpb, axis=0)
        rot = jnp.where(neg_half, -pltpu.roll(y, HEAD_DIM - HEAD_DIM // 4, 1),
                        pltpu.roll(y, HEAD_DIM // 4, 1))
        return y * table(ca_ref, cb_ref) + rot * table(sa_ref, sb_ref)

    @pl.when(g == 0)
    def _prepare_keys():
        def kblock(b, carry):
            r = pl.ds(pl.multiple_of(b * W, W), W)
            kn_ref[r, :] = rope(_rms_gain(k_ref[r, :].astype(F32), kg_ref[...]), b).astype(BF16)
            return carry
        lax.fori_loop(0, nblk, kblock, 0, unroll=2)
        for c in range(n_ctx // W):
            r = slice(n_x + c * W, n_x + (c + 1) * W)
            kn_ref[r, :] = _rms_gain(k_ref[r, :].astype(F32), kg_ref[...]).astype(BF16)

    sink = sink_ref[pl.program_id(0) * group + g]
    kc = kn_ref[n_x:n_x + n_ctx, :]
    vc = v_ref[n_x:n_x + n_ctx, :]
    j_minus_i = (lax.broadcasted_iota(jnp.int32, (W, 3 * W), 1)
                 - lax.broadcasted_iota(jnp.int32, (W, 3 * W), 0))

    def qblock(b, carry):
        r = pl.ds(pl.multiple_of(b * W, W), W)
        qn = (rope(_rms_gain(q_ref[r, :].astype(F32), qg_ref[...]), b) * scale).astype(BF16)
        a = jnp.clip((b - 1) * W, 0, n_x - 3 * W)
        krows = pl.ds(pl.multiple_of(a, W), 3 * W)
        s_loc = lax.dot_general(qn, kn_ref[krows, :], _NT, preferred_element_type=F32)
        dist = j_minus_i + (a - b * W)
        s_loc = jnp.where((dist >= -W) & (dist <= W), s_loc, NEG_INF)
        s_ctx = lax.dot_general(qn, kc, _NT, preferred_element_type=F32)
        o = _softmax_pv([(s_loc, v_ref[krows, :]), (s_ctx, vc)], extra_logit=sink)
        o_ref[r, :] = o.astype(o_ref.dtype)
        return carry

    lax.fori_loop(0, nblk, qblock, 0, unroll=2)

    for c in range(n_ctx // W):
        r = slice(n_x + c * W, n_x + (c + 1) * W)
        qc = (_rms_gain(q_ref[r, :].astype(F32), qg_ref[...]) * scale).astype(BF16)
        s = lax.dot_general(qc, kc, _NT, preferred_element_type=F32)
        o_ref[r, :] = _softmax_pv([(s, vc)], extra_logit=sink).astype(o_ref.dtype)


def _rope_tables(rows):
    axis_dim = HEAD_DIM // 2
    inv = ROPE_BASE ** (-jnp.arange(0, axis_dim, 2, dtype=F32) / axis_dim)
    ar = jnp.arange(rows, dtype=F32)[:, None] * inv
    ac = jnp.arange(GRID_W, dtype=F32)[:, None] * inv
    zr = jnp.zeros((rows, axis_dim), F32)
    zc = jnp.zeros((GRID_W, axis_dim), F32)

    def row_tab(f):
        return jnp.concatenate([f(ar), f(ar), zr], axis=-1)

    def col_tab(f):
        return jnp.concatenate([zc, f(ac), f(ac)], axis=-1)

    return row_tab(jnp.cos), row_tab(jnp.sin), col_tab(jnp.cos), col_tab(jnp.sin)


def _wa_attention(qkv, q_gain, k_gain, sink, *, n_x, n_ctx, n_heads):
    mtot = qkv.shape[0]
    H, KVH = n_heads, WA_KV_HEADS
    G = H // KVH
    W = WA_WINDOW
    assert W % GRID_W == 0 and n_x % W == 0 and n_x >= 3 * W and n_ctx % W == 0 and W == HEAD_DIM
    ca, sa, cb, sb = _rope_tables(n_x // GRID_W)
    blk = (mtot, HEAD_DIM)
    full = lambda arr: pl.BlockSpec(arr.shape, lambda kv, g: (0,) * arr.ndim)
    nbytes = 8 * mtot * HEAD_DIM * 2 + mtot * HEAD_DIM * 2 + 4 * (ca.size + cb.size) * 4
    return pl.pallas_call(
        functools.partial(_wa_body, n_x=n_x, n_ctx=n_ctx, group=G),
        grid=(KVH, G),
        in_specs=[
            pl.BlockSpec(blk, lambda kv, g: (0, kv * G + g)),
            pl.BlockSpec(blk, lambda kv, g: (0, H + kv)),
            pl.BlockSpec(blk, lambda kv, g: (0, H + KVH + kv)),
            pl.BlockSpec((1, HEAD_DIM), lambda kv, g: (0, 0)),
            pl.BlockSpec((1, HEAD_DIM), lambda kv, g: (0, 0)),
            full(ca), full(sa), full(cb), full(sb),
            pl.BlockSpec(memory_space=pltpu.SMEM),
        ],
        out_specs=pl.BlockSpec(blk, lambda kv, g: (0, kv * G + g)),
        out_shape=jax.ShapeDtypeStruct((mtot, H * HEAD_DIM), BF16),
        scratch_shapes=[pltpu.VMEM(blk, BF16)],
        compiler_params=pltpu.CompilerParams(
            dimension_semantics=("arbitrary", "arbitrary"), vmem_limit_bytes=_vmem_limit(nbytes)),
        name="wa_attention",
    )(qkv, qkv, qkv, q_gain.reshape(1, HEAD_DIM), k_gain.reshape(1, HEAD_DIM), ca, sa, cb, sb,
      sink.astype(F32))


def _sg_body(u_ref, v_ref, vg_ref, ws_ref, bs_ref, o_ref, *, n_chunks, groups):
    P = SG_CHUNK
    gw = v_ref.shape[1] // groups
    vn = _rms_gain(v_ref[...].astype(F32), vg_ref[...]).astype(BF16)
    for g in range(groups):
        wg = ws_ref[g].astype(BF16)
        cols = slice(g * gw, (g + 1) * gw)
        for c in range(n_chunks):
            rows = slice(c * P, (c + 1) * P)
            sv = jnp.dot(wg, vn[rows, cols], preferred_element_type=F32) + bs_ref[g]
            o_ref[rows, cols] = (u_ref[rows, cols].astype(F32) * sv).astype(o_ref.dtype)


def _spatial_gate(uv, v_gain, w_s, b_s):
    mtot, two_d = uv.shape
    D = two_d // 2
    G, P = SG_GROUPS, SG_CHUNK
    gw = D // G
    assert gw % LANES == 0 and mtot % P == 0
    n_chunks = _largest_divisor(mtot // P, (3, 2, 1))
    bm = n_chunks * P
    bias = jnp.broadcast_to(b_s.astype(F32)[:, :, None], (G, P, gw))
    nbytes = 6 * bm * D * 2 + bm * D * 6 + 2 * (w_s.size + bias.size) * 4
    return pl.pallas_call(
        functools.partial(_sg_body, n_chunks=n_chunks, groups=G),
        grid=(mtot // bm,),
        in_specs=[
            pl.BlockSpec((bm, D), lambda i: (i, 0)),
            pl.BlockSpec((bm, D), lambda i: (i, 1)),
            pl.BlockSpec((1, D), lambda i: (0, 0)),
            pl.BlockSpec(w_s.shape, lambda i: (0, 0, 0)),
            pl.BlockSpec(bias.shape, lambda i: (0, 0, 0)),
        ],
        out_specs=pl.BlockSpec((bm, D), lambda i: (i, 0)),
        out_shape=jax.ShapeDtypeStruct((mtot, D), BF16),
        compiler_params=pltpu.CompilerParams(
            dimension_semantics=("arbitrary",), vmem_limit_bytes=_vmem_limit(nbytes)),
        name="spatial_gate",
    )(uv, uv, v_gain.reshape(1, D), w_s, bias)


def _ada_modulation(cond, a, b, bias, layer):
    t = _matmul(cond, a, layer, out_dtype=F32, prologue="silu", name="ada_down")
    return _matmul(t, b, layer, out_dtype=F32, epilogue="bias", vec=bias[:, None, :],
                   bn=_largest_divisor(b.shape[2], (2048, 1024, 512, 256, 128)), name="ada_up")


def kernel(x, c, ctx, c_ctx, ada_a, ada_b, ada_bias, norm_mix, norm_mlp, mlp_w1, mlp_w2,
           na_w_qkv, na_w_o, na_q_norm, na_k_norm, na_rel_bias,
           wa_w_qkv, wa_w_o, wa_q_norm, wa_k_norm, wa_sink,
           sg_w_in, sg_v_norm, sg_w_s, sg_b_s, sg_w_out):
    assert x.shape[0] == 1 and ctx.shape[0] == 1 and c.shape[0] == 1
    n_x, D = x.shape[1], x.shape[2]
    n_ctx = ctx.shape[1]
    depth = ada_a.shape[0]
    n_heads = D // HEAD_DIM
    n_mixers = 3

    xc = jnp.concatenate([x[0], ctx[0]], axis=0)
    cond = jnp.zeros((16, D), F32).at[0].set(c[0]).at[1].set(c_ctx)

    for i in range(depth):
        last = i == depth - 1
        kind, j = i % n_mixers, i // n_mixers
        mods = _ada_modulation(cond, ada_a, ada_b, ada_bias, i)
        gate = dict(epilogue="gate_res", vec=mods, n_x=n_x, out_dtype=F32,
                    m_rows=n_x if last else None)

        h = _norm_modulate(xc, norm_mix[i], mods, shift_col=0, scale_col=1, n_x=n_x)
        if kind == 0:
            qkv = _matmul(h, na_w_qkv, j, out_dtype=BF16, name="na_qkv")
            o = _na_attention(qkv, na_q_norm[j], na_k_norm[j], na_rel_bias[j], n_x=n_x, n_ctx=n_ctx)
            xc = _matmul(o, na_w_o, j, res=xc, vec_col=2, name="na_out", **gate)
        elif kind == 1:
            qkv = _matmul(h, wa_w_qkv, j, out_dtype=BF16, name="wa_qkv")
            o = _wa_attention(qkv, wa_q_norm[j], wa_k_norm[j], wa_sink[j], n_x=n_x, n_ctx=n_ctx,
                              n_heads=n_heads)
            xc = _matmul(o, wa_w_o, j, res=xc, vec_col=2, name="wa_out", **gate)
        else:
            uv = _matmul(h, sg_w_in, j, out_dtype=BF16, epilogue="gelu", name="sg_in")
            z = _spatial_gate(uv, sg_v_norm[j], sg_w_s[j], sg_b_s[j])
            xc = _matmul(z, sg_w_out, j, res=xc, vec_col=2, name="sg_out", **gate)

        h2 = _norm_modulate(xc, norm_mlp[i], mods, shift_col=3, scale_col=4, n_x=n_x)
        hid = _matmul(h2, mlp_w1, i, out_dtype=BF16, epilogue="relu2", name="mlp_up")
        xc = _matmul(hid, mlp_w2, i, res=xc, vec_col=5, name="mlp_down", **gate)
    return xc[None]
```

```python
import functools

import numpy as np
import jax
import jax.numpy as jnp
from jax import lax
from jax.experimental import pallas as pl
from jax.experimental.pallas import tpu as pltpu

GRID_W = 64
HEAD_DIM = 128
NA_KH = 8
NA_KW = 16
WA_KV_HEADS = 8
WA_WINDOW = 128
ROPE_BASE = 10000.0
SG_GROUPS = 8
SG_CHUNK = 128
N_MOD = 6
EPS = 1e-6
NEG_INF = -1e30
LOG2E = np.float32(np.log2(np.e))
PAIRS_PER_BODY = 4
SCORE_SLOTS = 4

LANES = 128
V7X_VMEM_BYTES = 64 * 1024 * 1024
VMEM_CAP = V7X_VMEM_BYTES - 6 * 1024 * 1024
VMEM_SLACK = 2 * 1024 * 1024

F32 = jnp.float32
BF16 = jnp.bfloat16
_NT = (((1,), (1,)), ((), ()))


def _vmem_limit(nbytes):
    return int(min(VMEM_CAP, nbytes + 4 * VMEM_SLACK))


def _largest_divisor(n, candidates):
    for c in candidates:
        if n % c == 0:
            return c
    raise ValueError(f"no tile in {candidates} divides {n}")


def _mm_body(*refs, nk, bm, epilogue, prologue, n_x, norm_tiles):
    x_ref, w_ref = refs[0], refs[1]
    pos = 2
    res_ref = vec_ref = None
    if epilogue == "gate_res":
        res_ref, vec_ref = refs[2], refs[3]
        pos = 4
    elif epilogue in ("bias", "head_norm"):
        vec_ref = refs[2]
        pos = 3
    o_ref, wbf_ref = refs[pos], refs[pos + 1]
    acc_ref = refs[pos + 2] if nk > 1 else None
    k = pl.program_id(1)
    m = pl.program_id(2)

    @pl.when(m == 0)
    def _cast_weights():
        wbf_ref[...] = w_ref[...].astype(BF16)

    if nk > 1:
        rows = pl.ds(pl.multiple_of(m * bm, bm), bm)

        @pl.when(k == 0)
        def _zero():
            acc_ref[rows, :] = jnp.zeros((bm, acc_ref.shape[1]), F32)

    xv = x_ref[...]
    if prologue == "silu":
        xv = xv * jax.nn.sigmoid(xv)
    part = jnp.dot(xv.astype(BF16), wbf_ref[...], preferred_element_type=F32)

    def finish(acc):
        if epilogue == "relu2":
            y = jnp.square(jnp.maximum(acc, 0.0))
        elif epilogue == "gelu":
            y = 0.5 * acc * (1.0 + lax.erf(acc * np.float32(np.sqrt(0.5))))
        elif epilogue == "gate_res":
            rows = m * bm + lax.broadcasted_iota(jnp.int32, (bm, 1), 0)
            gate = jnp.where(rows >= n_x, vec_ref[1:2, :], vec_ref[0:1, :])
            y = res_ref[...] + gate * acc
        elif epilogue == "bias":
            y = acc + vec_ref[...]
        elif epilogue == "head_norm":
            n = pl.program_id(0)

            @pl.when(n < norm_tiles[1])
            def _normed():
                gain = vec_ref[pl.ds((n >= norm_tiles[0]).astype(jnp.int32), 1), :]
                for t in range(acc.shape[1] // HEAD_DIM):
                    cols = slice(t * HEAD_DIM, (t + 1) * HEAD_DIM)
                    o_ref[:, cols] = _rms_gain(acc[:, cols], gain).astype(o_ref.dtype)

            @pl.when(n >= norm_tiles[1])
            def _plain():
                o_ref[...] = acc.astype(o_ref.dtype)
            return
        else:
            y = acc
        o_ref[...] = y.astype(o_ref.dtype)

    if nk == 1:
        finish(part)
    else:
        acc_ref[rows, :] = acc_ref[rows, :] + part

        @pl.when(k == nk - 1)
        def _emit():
            finish(acc_ref[rows, :])


def _matmul(x, w, layer, *, out_dtype, epilogue=None, prologue=None, res=None, vec=None, vec_col=0,
            n_x=0, m_rows=None, norm_cols=None, bm=None, bn=None, bk=None, name="mm"):
    M = x.shape[0] if m_rows is None else m_rows
    _, K, N = w.shape
    bn = bn or _largest_divisor(int(np.gcd.reduce((N,) + tuple(norm_cols or ()))), (512, 256, 128))
    bk = bk or _largest_divisor(K, (4096, 2048, 1024, 512, 256, 128))
    nk = K // bk
    last_k = nk - 1
    out_bytes = jnp.dtype(out_dtype).itemsize

    def vmem_bytes(bm_):
        n = 2 * bm_ * bk * x.dtype.itemsize + 2 * bk * bn * 4 + bk * bn * 2 + 2 * bm_ * bn * out_bytes
        if epilogue == "gate_res":
            n += 2 * bm_ * bn * 4 + 2 * vec.shape[0] * bn * 4
        if nk > 1:
            n += M * bn * 4
        return n

    if bm is None:
        bm = next(c for c in (1024, 768, 512, 384, 256, 128, 16)
                  if M % c == 0 and vmem_bytes(c) <= VMEM_CAP - VMEM_SLACK)

    def out_rows(k, m):
        return m if nk == 1 else jnp.where(k == last_k, m, 0)

    in_specs = [
        pl.BlockSpec((bm, bk), lambda n, k, m: (m, k)),
        pl.BlockSpec((None, bk, bn), lambda n, k, m: (layer, k, n)),
    ]
    args = [x, w]
    if epilogue == "gate_res":
        in_specs.append(pl.BlockSpec((bm, bn), lambda n, k, m: (out_rows(k, m), n)))
        in_specs.append(pl.BlockSpec((vec.shape[0], bn), lambda n, k, m: (0, vec_col * (N // bn) + n)))
        args += [res, vec]
    elif epilogue == "bias":
        in_specs.append(pl.BlockSpec((None, 1, bn), lambda n, k, m: (layer, 0, n)))
        args.append(vec)
    norm_tiles = None
    if epilogue == "head_norm":
        in_specs.append(pl.BlockSpec(vec.shape, lambda n, k, m: (0, 0)))
        args.append(vec)
        assert norm_cols[0] % bn == 0 and norm_cols[1] % bn == 0 and bn % HEAD_DIM == 0
        norm_tiles = (norm_cols[0] // bn, norm_cols[1] // bn)
    scratch = [pltpu.VMEM((bk, bn), BF16)]
    if nk > 1:
        scratch.append(pltpu.VMEM((M, bn), F32))
    body = functools.partial(_mm_body, nk=nk, bm=bm, epilogue=epilogue, prologue=prologue, n_x=n_x,
                             norm_tiles=norm_tiles)
    return pl.pallas_call(
        body,
        grid=(N // bn, nk, M // bm),
        in_specs=in_specs,
        out_specs=pl.BlockSpec((bm, bn), lambda n, k, m: (out_rows(k, m), n)),
        out_shape=jax.ShapeDtypeStruct((M, N), out_dtype),
        scratch_shapes=scratch,
        compiler_params=pltpu.CompilerParams(
            dimension_semantics=("arbitrary", "arbitrary", "arbitrary"),
            vmem_limit_bytes=_vmem_limit(vmem_bytes(bm))),
        name=name,
    )(*args)


def _norm_body(x_ref, g_ref, sh_ref, sc_ref, o_ref, *, nx_tiles):
    t = (pl.program_id(0) >= nx_tiles).astype(jnp.int32)
    x = x_ref[...]
    y = x * lax.rsqrt(jnp.mean(x * x, axis=-1, keepdims=True) + EPS)
    y = y * g_ref[...]
    shift = sh_ref[pl.ds(t, 1), :]
    scale = sc_ref[pl.ds(t, 1), :]
    o_ref[...] = (y * (1.0 + scale) + shift).astype(o_ref.dtype)


def _norm_modulate(x, gain, mods, *, shift_col, scale_col, n_x, m_rows=None):
    M = x.shape[0] if m_rows is None else m_rows
    D = x.shape[1]
    n_ctx = x.shape[0] - n_x
    bm = _largest_divisor(int(np.gcd(n_x, n_ctx)), (256, 128, 64, 32, 16))
    nbytes = 2 * bm * D * 4 + 2 * bm * D * 2 + 6 * 8 * D * 4
    return pl.pallas_call(
        functools.partial(_norm_body, nx_tiles=n_x // bm),
        grid=(M // bm,),
        in_specs=[
            pl.BlockSpec((bm, D), lambda i: (i, 0)),
            pl.BlockSpec((1, D), lambda i: (0, 0)),
            pl.BlockSpec((mods.shape[0], D), lambda i: (0, shift_col)),
            pl.BlockSpec((mods.shape[0], D), lambda i: (0, scale_col)),
        ],
        out_specs=pl.BlockSpec((bm, D), lambda i: (i, 0)),
        out_shape=jax.ShapeDtypeStruct((M, D), BF16),
        compiler_params=pltpu.CompilerParams(
            dimension_semantics=("arbitrary",), vmem_limit_bytes=_vmem_limit(nbytes)),
        name="norm_modulate",
    )(x, gain.reshape(1, D), mods, mods)


def _rms_gain(xf, gain):
    return (xf * lax.rsqrt(jnp.mean(xf * xf, axis=-1, keepdims=True) + EPS)) * gain


def _softmax_pv(parts, extra_logit=None):
    m_tile = None
    for s, _ in parts:
        for t in range(s.shape[1] // LANES):
            st = s[:, t * LANES:(t + 1) * LANES]
            m_tile = st if m_tile is None else jnp.maximum(m_tile, st)
    m = jnp.max(m_tile, axis=-1, keepdims=True)
    if extra_logit is not None:
        m = jnp.maximum(m, extra_logit)
    acc = None
    for s, v in parts:
        v_ones = jnp.concatenate([v, jnp.ones_like(v)], axis=1)
        ov = jnp.dot(jnp.exp2(s - m).astype(BF16), v_ones, preferred_element_type=F32)
        acc = ov if acc is None else acc + ov
    d = acc.shape[1] // 2
    o, l = acc[:, :d], acc[:, d:]
    if extra_logit is not None:
        l = l + jnp.exp2(extra_logit - m)
    return o / l


def _na_body(q_ref, k_ref, v_ref, tb_ref, o_ref, sl_ref, sc_ref, *, n_x, n_ctx, rows, kh):
    W = GRID_W
    nkr = kh + 2
    n_steps = rows // 2
    kc = k_ref[n_x:n_x + n_ctx, :]
    vc = v_ref[n_x:n_x + n_ctx, :]

    def windows(i):
        r = 2 * i
        a = jnp.clip(r - kh // 2, 0, rows - nkr)
        return (r, a, pl.ds(pl.multiple_of(r * W, 2 * W), 2 * W),
                pl.ds(pl.multiple_of(a * W, 2 * W), nkr * W))

    def scores(i, slot):
        r, a, qrows, krows = windows(i)
        q2 = q_ref[qrows, :]
        s_loc = lax.dot_general(q2, k_ref[krows, :], _NT, preferred_element_type=F32)
        bias_rows = []
        for j in range(2):
            qr = r + j
            r0 = jnp.clip(qr - kh // 2, 0, rows - kh)
            pieces = []
            for t in range(nkr // 2):
                kr0 = a + 2 * t
                in0 = (kr0 >= r0) & (kr0 < r0 + kh)
                in1 = (kr0 + 1 >= r0) & (kr0 + 1 < r0 + kh)
                variant = jnp.where(in0 & in1, 0, jnp.where(in0, 1, 2))
                plane = jnp.where(in0 | in1, jnp.clip(kr0 - qr + kh, 0, 2 * kh - 1), 2 * kh)
                pieces.append(tb_ref[0, variant, plane])
            bias_rows.append(jnp.concatenate(pieces, axis=1))
        sl_ref[slot] = s_loc + jnp.concatenate(bias_rows, axis=0)
        sc_ref[slot] = lax.dot_general(q2, kc, _NT, preferred_element_type=F32)

    def attend(i, slot):
        _, _, qrows, krows = windows(i)
        o = _softmax_pv([(sl_ref[slot], v_ref[krows, :]), (sc_ref[slot], vc)])
        o_ref[qrows, :] = o.astype(o_ref.dtype)

    scores(0, 0)
    scores(1, 1)

    def pairs(u, carry):
        for ph in range(PAIRS_PER_BODY):
            t = 2 * (PAIRS_PER_BODY * u + ph)
            cur = 2 * (ph % 2)
            for j in range(2):
                scores(jnp.minimum(t + 2 + j, n_steps - 1), 2 - cur + j)
                attend(t + j, cur + j)
        return carry

    lax.fori_loop(0, n_steps // (2 * PAIRS_PER_BODY), pairs, 0)

    ctx_rows = slice(n_x, n_x + n_ctx)
    s = lax.dot_general(q_ref[ctx_rows, :], kc, _NT, preferred_element_type=F32)
    o_ref[ctx_rows, :] = _softmax_pv([(s, vc)]).astype(o_ref.dtype)

def _na_bias_planes(rel_bias):
    W, kw = GRID_W, NA_KW
    qc = np.arange(W)[:, None]
    kc = np.arange(W)[None, :]
    win = np.clip(qc - kw // 2, 0, W - kw)
    ok = (kc >= win) & (kc < win + kw)
    dc = np.clip(kc - qc, -(kw - 1), kw - 1) + kw - 1
    t = jnp.where(jnp.asarray(ok), rel_bias[:, :, dc].astype(F32) * LOG2E, NEG_INF)
    neg = jnp.full(t.shape[:1] + (1,) + t.shape[2:], NEG_INF, F32)
    text = jnp.concatenate([neg, t, neg, neg], axis=1)
    left, right = text[:, :-1], text[:, 1:]
    both = jnp.concatenate([left, right], axis=-1)
    left_only = jnp.concatenate([left, jnp.full_like(right, NEG_INF)], axis=-1)
    right_only = jnp.concatenate([jnp.full_like(left, NEG_INF), right], axis=-1)
    return jnp.stack([both, left_only, right_only], axis=1)


def _na_attention(qkv, rel_bias, *, n_x, n_ctx):
    mtot, width = qkv.shape
    H = width // (3 * HEAD_DIM)
    rows = n_x // GRID_W
    kh = min(NA_KH, rows)
    assert GRID_W * 2 == LANES and rows % (4 * PAIRS_PER_BODY) == 0 and rows >= kh + 2
    assert kh % 2 == 0 and PAIRS_PER_BODY % 2 == 0 and SCORE_SLOTS == 4
    planes = _na_bias_planes(rel_bias)
    blk = (mtot, HEAD_DIM)
    nq, nk = 2 * GRID_W, (kh + 2) * GRID_W
    nbytes = (8 * mtot * HEAD_DIM * 2 + 2 * planes[0].size * 4
              + SCORE_SLOTS * nq * (nk + n_ctx) * 4)
    return pl.pallas_call(
        functools.partial(_na_body, n_x=n_x, n_ctx=n_ctx, rows=rows, kh=kh),
        grid=(H,),
        in_specs=[
            pl.BlockSpec(blk, lambda h: (0, h)),
            pl.BlockSpec(blk, lambda h: (0, H + h)),
            pl.BlockSpec(blk, lambda h: (0, 2 * H + h)),
            pl.BlockSpec((1,) + planes.shape[1:], lambda h: (h, 0, 0, 0, 0)),
        ],
        out_specs=pl.BlockSpec(blk, lambda h: (0, h)),
        out_shape=jax.ShapeDtypeStruct((mtot, H * HEAD_DIM), BF16),
        scratch_shapes=[pltpu.VMEM((SCORE_SLOTS, nq, nk), F32),
                        pltpu.VMEM((SCORE_SLOTS, nq, n_ctx), F32)],
        compiler_params=pltpu.CompilerParams(
            dimension_semantics=("arbitrary",), vmem_limit_bytes=_vmem_limit(nbytes)),
        name="na_attention",
    )(qkv, qkv, qkv, planes)


def _wa_body(q_ref, k_ref, v_ref, ca_ref, sa_ref, cb_ref, sb_ref, band_ref, sink_ref,
             o_ref, kn_ref, sl_ref, sc_ref, *, n_x, n_ctx, group):
    W = WA_WINDOW
    rpb = W // GRID_W
    nblk = n_x // W
    g = pl.program_id(1)
    lane = lax.broadcasted_iota(jnp.int32, (W, HEAD_DIM), 1)
    neg_half = (lane % (HEAD_DIM // 2)) < HEAD_DIM // 4

    def rope(y, b):
        def table(row_ref, col_ref):
            row_part = [jnp.broadcast_to(row_ref[pl.ds(rpb * b + j, 1), :], (GRID_W, HEAD_DIM))
                        for j in range(rpb)]
            return jnp.concatenate(row_part, axis=0) + jnp.concatenate([col_ref[...]] * rpb, axis=0)
        rot = jnp.where(neg_half, -pltpu.roll(y, HEAD_DIM - HEAD_DIM // 4, 1),
                        pltpu.roll(y, HEAD_DIM // 4, 1))
        return y * table(ca_ref, cb_ref) + rot * table(sa_ref, sb_ref)

    @pl.when(g == 0)
    def _prepare_keys():
        def kblock(b, carry):
            r = pl.ds(pl.multiple_of(b * W, W), W)
            kn_ref[r, :] = rope(k_ref[r, :].astype(F32), b).astype(BF16)
            return carry
        lax.fori_loop(0, nblk, kblock, 0, unroll=2)
        kn_ref[n_x:n_x + n_ctx, :] = k_ref[n_x:n_x + n_ctx, :]

    sink = sink_ref[pl.program_id(0) * group + g] * LOG2E
    kc = kn_ref[n_x:n_x + n_ctx, :]
    vc = v_ref[n_x:n_x + n_ctx, :]

    def windows(b):
        a = jnp.clip((b - 1) * W, 0, n_x - 3 * W)
        return pl.ds(pl.multiple_of(b * W, W), W), pl.ds(pl.multiple_of(a, W), 3 * W)

    def scores(b, slot):
        r, krows = windows(b)
        qn = rope(q_ref[r, :].astype(F32), b).astype(BF16)
        s_loc = lax.dot_general(qn, kn_ref[krows, :], _NT, preferred_element_type=F32)
        edge = jnp.where(b == 0, 0, jnp.where(b == nblk - 1, 2, 1))
        sl_ref[slot] = s_loc + band_ref[edge]
        sc_ref[slot] = lax.dot_general(qn, kc, _NT, preferred_element_type=F32)

    def attend(b, slot):
        r, krows = windows(b)
        o = _softmax_pv([(sl_ref[slot], v_ref[krows, :]), (sc_ref[slot], vc)], extra_logit=sink)
        o_ref[r, :] = o.astype(o_ref.dtype)

    scores(0, 0)
    scores(1, 1)

    def pairs(u, carry):
        for ph in range(PAIRS_PER_BODY):
            t = 2 * (PAIRS_PER_BODY * u + ph)
            cur = 2 * (ph % 2)
            for j in range(2):
                scores(jnp.minimum(t + 2 + j, nblk - 1), 2 - cur + j)
                attend(t + j, cur + j)
        return carry

    lax.fori_loop(0, nblk // (2 * PAIRS_PER_BODY), pairs, 0)

    for c in range(n_ctx // W):
        r = slice(n_x + c * W, n_x + (c + 1) * W)
        s = lax.dot_general(q_ref[r, :], kc, _NT, preferred_element_type=F32)
        o_ref[r, :] = _softmax_pv([(s, vc)], extra_logit=sink).astype(o_ref.dtype)


def _rope_tables(rows):
    axis_dim = HEAD_DIM // 2
    inv = ROPE_BASE ** (-jnp.arange(0, axis_dim, 2, dtype=F32) / axis_dim)
    ar = jnp.arange(rows, dtype=F32)[:, None] * inv
    ac = jnp.arange(GRID_W, dtype=F32)[:, None] * inv
    zr = jnp.zeros((rows, axis_dim), F32)
    zc = jnp.zeros((GRID_W, axis_dim), F32)

    def row_tab(f):
        return jnp.concatenate([f(ar), f(ar), zr], axis=-1)

    def col_tab(f):
        return jnp.concatenate([zc, f(ac), f(ac)], axis=-1)

    return row_tab(jnp.cos), row_tab(jnp.sin), col_tab(jnp.cos), col_tab(jnp.sin)


def _wa_band_bias():
    W = WA_WINDOW
    dist = np.arange(3 * W)[None, :] - np.arange(W)[:, None]
    return np.stack([np.where(np.abs(dist + d) <= W, 0.0, NEG_INF) for d in (0, -W, -2 * W)]
                    ).astype(np.float32)


def _wa_attention(qkv, sink, *, n_x, n_ctx, n_heads):
    mtot = qkv.shape[0]
    H, KVH = n_heads, WA_KV_HEADS
    G = H // KVH
    W = WA_WINDOW
    assert W % GRID_W == 0 and n_x % (2 * PAIRS_PER_BODY * W) == 0 and n_ctx % W == 0
    assert W == HEAD_DIM and PAIRS_PER_BODY % 2 == 0 and SCORE_SLOTS == 4
    ca, sa, cb, sb = _rope_tables(n_x // GRID_W)
    band = jnp.asarray(_wa_band_bias())
    blk = (mtot, HEAD_DIM)
    full = lambda arr: pl.BlockSpec(arr.shape, lambda kv, g: (0,) * arr.ndim)
    nbytes = (8 * mtot * HEAD_DIM * 2 + mtot * HEAD_DIM * 2 + 4 * (ca.size + cb.size) * 4
              + 2 * band.size * 4 + SCORE_SLOTS * W * (3 * W + n_ctx) * 4)
    return pl.pallas_call(
        functools.partial(_wa_body, n_x=n_x, n_ctx=n_ctx, group=G),
        grid=(KVH, G),
        in_specs=[
            pl.BlockSpec(blk, lambda kv, g: (0, kv * G + g)),
            pl.BlockSpec(blk, lambda kv, g: (0, H + kv)),
            pl.BlockSpec(blk, lambda kv, g: (0, H + KVH + kv)),
            full(ca), full(sa), full(cb), full(sb), full(band),
            pl.BlockSpec(memory_space=pltpu.SMEM),
        ],
        out_specs=pl.BlockSpec(blk, lambda kv, g: (0, kv * G + g)),
        out_shape=jax.ShapeDtypeStruct((mtot, H * HEAD_DIM), BF16),
        scratch_shapes=[pltpu.VMEM(blk, BF16), pltpu.VMEM((SCORE_SLOTS, W, 3 * W), F32),
                        pltpu.VMEM((SCORE_SLOTS, W, n_ctx), F32)],
        compiler_params=pltpu.CompilerParams(
            dimension_semantics=("arbitrary", "arbitrary"), vmem_limit_bytes=_vmem_limit(nbytes)),
        name="wa_attention",
    )(qkv, qkv, qkv, ca, sa, cb, sb, band, sink.astype(F32))


def _sg_body(u_ref, v_ref, vg_ref, ws_ref, bs_ref, o_ref, *, n_chunks, groups):
    P = SG_CHUNK
    gw = v_ref.shape[1] // groups
    vn = _rms_gain(v_ref[...].astype(F32), vg_ref[...]).astype(BF16)
    for g in range(groups):
        wg = ws_ref[g].astype(BF16)
        cols = slice(g * gw, (g + 1) * gw)
        for c in range(n_chunks):
            rows = slice(c * P, (c + 1) * P)
            sv = jnp.dot(wg, vn[rows, cols], preferred_element_type=F32) + bs_ref[g]
            o_ref[rows, cols] = (u_ref[rows, cols].astype(F32) * sv).astype(o_ref.dtype)


def _spatial_gate(uv, v_gain, w_s, b_s):
    mtot, two_d = uv.shape
    D = two_d // 2
    G, P = SG_GROUPS, SG_CHUNK
    gw = D // G
    assert gw % LANES == 0 and mtot % P == 0
    n_chunks = _largest_divisor(mtot // P, (3, 2, 1))
    bm = n_chunks * P
    bias = jnp.broadcast_to(b_s.astype(F32)[:, :, None], (G, P, gw))
    nbytes = 6 * bm * D * 2 + bm * D * 6 + 2 * (w_s.size + bias.size) * 4
    return pl.pallas_call(
        functools.partial(_sg_body, n_chunks=n_chunks, groups=G),
        grid=(mtot // bm,),
        in_specs=[
            pl.BlockSpec((bm, D), lambda i: (i, 0)),
            pl.BlockSpec((bm, D), lambda i: (i, 1)),
            pl.BlockSpec((1, D), lambda i: (0, 0)),
            pl.BlockSpec(w_s.shape, lambda i: (0, 0, 0)),
            pl.BlockSpec(bias.shape, lambda i: (0, 0, 0)),
        ],
        out_specs=pl.BlockSpec((bm, D), lambda i: (i, 0)),
        out_shape=jax.ShapeDtypeStruct((mtot, D), BF16),
        compiler_params=pltpu.CompilerParams(
            dimension_semantics=("arbitrary",), vmem_limit_bytes=_vmem_limit(nbytes)),
        name="spatial_gate",
    )(uv, uv, v_gain.reshape(1, D), w_s, bias)


def _head_gains(q_gain, k_gain):
    scale = np.float32(HEAD_DIM ** -0.5) * LOG2E
    rows = jnp.stack([q_gain.astype(F32) * scale, k_gain.astype(F32)])
    return jnp.concatenate([rows, jnp.zeros((6, HEAD_DIM), F32)], axis=0)


def _ada_modulation(cond, a, b, bias, layer):
    t = _matmul(cond, a, layer, out_dtype=F32, prologue="silu", name="ada_down")
    return _matmul(t, b, layer, out_dtype=F32, epilogue="bias", vec=bias[:, None, :],
                   bn=_largest_divisor(b.shape[2], (2048, 1024, 512, 256, 128)), name="ada_up")


def kernel(x, c, ctx, c_ctx, ada_a, ada_b, ada_bias, norm_mix, norm_mlp, mlp_w1, mlp_w2,
           na_w_qkv, na_w_o, na_q_norm, na_k_norm, na_rel_bias,
           wa_w_qkv, wa_w_o, wa_q_norm, wa_k_norm, wa_sink,
           sg_w_in, sg_v_norm, sg_w_s, sg_b_s, sg_w_out):
    assert x.shape[0] == 1 and ctx.shape[0] == 1 and c.shape[0] == 1
    n_x, D = x.shape[1], x.shape[2]
    n_ctx = ctx.shape[1]
    depth = ada_a.shape[0]
    n_heads = D // HEAD_DIM
    n_mixers = 3

    xc = jnp.concatenate([x[0], ctx[0]], axis=0)
    cond = jnp.zeros((16, D), F32).at[0].set(c[0]).at[1].set(c_ctx)

    for i in range(depth):
        last = i == depth - 1
        kind, j = i % n_mixers, i // n_mixers
        mods = _ada_modulation(cond, ada_a, ada_b, ada_bias, i)
        gate = dict(epilogue="gate_res", vec=mods, n_x=n_x, out_dtype=F32,
                    m_rows=n_x if last else None)

        h = _norm_modulate(xc, norm_mix[i], mods, shift_col=0, scale_col=1, n_x=n_x)
        if kind == 0:
            qkv = _matmul(h, na_w_qkv, j, out_dtype=BF16, epilogue="head_norm",
                          vec=_head_gains(na_q_norm[j], na_k_norm[j]), norm_cols=(D, 2 * D),
                          name="na_qkv")
            o = _na_attention(qkv, na_rel_bias[j], n_x=n_x, n_ctx=n_ctx)
            xc = _matmul(o, na_w_o, j, res=xc, vec_col=2, name="na_out", **gate)
        elif kind == 1:
            qkv = _matmul(h, wa_w_qkv, j, out_dtype=BF16, epilogue="head_norm",
                          vec=_head_gains(wa_q_norm[j], wa_k_norm[j]),
                          norm_cols=(D, D + WA_KV_HEADS * HEAD_DIM), name="wa_qkv")
            o = _wa_attention(qkv, wa_sink[j], n_x=n_x, n_ctx=n_ctx, n_heads=n_heads)
            xc = _matmul(o, wa_w_o, j, res=xc, vec_col=2, name="wa_out", **gate)
        else:
            uv = _matmul(h, sg_w_in, j, out_dtype=BF16, epilogue="gelu", name="sg_in")
            z = _spatial_gate(uv, sg_v_norm[j], sg_w_s[j], sg_b_s[j])
            xc = _matmul(z, sg_w_out, j, res=xc, vec_col=2, name="sg_out", **gate)

        h2 = _norm_modulate(xc, norm_mlp[i], mods, shift_col=3, scale_col=4, n_x=n_x)
        hid = _matmul(h2, mlp_w1, i, out_dtype=BF16, epilogue="relu2", name="mlp_up")
        xc = _matmul(hid, mlp_w2, i, res=xc, vec_col=5, name="mlp_down", **gate)
    return xc[None]
```

```python
import functools

import numpy as np
import jax
import jax.numpy as jnp
from jax import lax
from jax.experimental import pallas as pl
from jax.experimental.pallas import tpu as pltpu

GRID_W = 64
HEAD_DIM = 128
NA_KH = 8
NA_KW = 16
WA_KV_HEADS = 8
WA_WINDOW = 128
ROPE_BASE = 10000.0
SG_GROUPS = 8
SG_CHUNK = 128
N_MOD = 6
EPS = 1e-6
NEG_INF = -1e30
LOG2E = np.float32(np.log2(np.e))
PAIRS_PER_BODY = 4
SCORE_SLOTS = 4

LANES = 128
V7X_VMEM_BYTES = 64 * 1024 * 1024
VMEM_CAP = V7X_VMEM_BYTES - 6 * 1024 * 1024
VMEM_SLACK = 2 * 1024 * 1024

F32 = jnp.float32
BF16 = jnp.bfloat16
_NT = (((1,), (1,)), ((), ()))


def _vmem_limit(nbytes):
    return int(min(VMEM_CAP, nbytes + 4 * VMEM_SLACK))


def _largest_divisor(n, candidates):
    for c in candidates:
        if n % c == 0:
            return c
    raise ValueError(f"no tile in {candidates} divides {n}")


def _mm_body(*refs, nk, bm, epilogue, prologue, n_x, norm_tiles):
    x_ref, w_ref = refs[0], refs[1]
    pos = 2
    res_ref = vec_ref = None
    if epilogue == "gate_res":
        res_ref, vec_ref = refs[2], refs[3]
        pos = 4
    elif epilogue in ("bias", "head_norm"):
        vec_ref = refs[2]
        pos = 3
    o_ref, wbf_ref = refs[pos], refs[pos + 1]
    acc_ref = refs[pos + 2] if nk > 1 else None
    k = pl.program_id(1)
    m = pl.program_id(2)

    @pl.when(m == 0)
    def _cast_weights():
        wbf_ref[...] = w_ref[...].astype(BF16)

    if nk > 1:
        rows = pl.ds(pl.multiple_of(m * bm, bm), bm)

        @pl.when(k == 0)
        def _zero():
            acc_ref[rows, :] = jnp.zeros((bm, acc_ref.shape[1]), F32)

    xv = x_ref[...]
    if prologue == "silu":
        xv = xv * jax.nn.sigmoid(xv)
    part = jnp.dot(xv.astype(BF16), wbf_ref[...], preferred_element_type=F32)

    def finish(acc):
        if epilogue == "relu2":
            y = jnp.square(jnp.maximum(acc, 0.0))
        elif epilogue == "gelu":
            y = 0.5 * acc * (1.0 + lax.erf(acc * np.float32(np.sqrt(0.5))))
        elif epilogue == "gate_res":
            rows = m * bm + lax.broadcasted_iota(jnp.int32, (bm, 1), 0)
            gate = jnp.where(rows >= n_x, vec_ref[1:2, :], vec_ref[0:1, :])
            y = res_ref[...] + gate * acc
        elif epilogue == "bias":
            y = acc + vec_ref[...]
        elif epilogue == "head_norm":
            n = pl.program_id(0)
            gain = vec_ref[pl.ds((n >= norm_tiles[0]).astype(jnp.int32), 1), :]
            for t in range(acc.shape[1] // HEAD_DIM):
                cols = slice(t * HEAD_DIM, (t + 1) * HEAD_DIM)
                a = acc[:, cols]
                o_ref[:, cols] = jnp.where(n < norm_tiles[1], _rms_gain(a, gain), a).astype(o_ref.dtype)
            return
        else:
            y = acc
        o_ref[...] = y.astype(o_ref.dtype)

    if nk == 1:
        finish(part)
    else:
        acc_ref[rows, :] = acc_ref[rows, :] + part

        @pl.when(k == nk - 1)
        def _emit():
            finish(acc_ref[rows, :])


def _matmul(x, w, layer, *, out_dtype, epilogue=None, prologue=None, res=None, vec=None, vec_col=0,
            n_x=0, m_rows=None, norm_cols=None, bm=None, bn=None, bk=None, name="mm"):
    M = x.shape[0] if m_rows is None else m_rows
    _, K, N = w.shape
    bn = bn or _largest_divisor(int(np.gcd.reduce((N,) + tuple(norm_cols or ()))), (512, 256, 128))
    bk = bk or _largest_divisor(K, (4096, 2048, 1024, 512, 256, 128))
    nk = K // bk
    last_k = nk - 1
    out_bytes = jnp.dtype(out_dtype).itemsize

    def vmem_bytes(bm_):
        n = 2 * bm_ * bk * x.dtype.itemsize + 2 * bk * bn * 4 + bk * bn * 2 + 2 * bm_ * bn * out_bytes
        if epilogue == "gate_res":
            n += 2 * bm_ * bn * 4 + 2 * vec.shape[0] * bn * 4
        if nk > 1:
            n += M * bn * 4
        return n

    if bm is None:
        bm = next(c for c in (1408, 1024, 768, 512, 384, 256, 128, 16)
                  if M % c == 0 and vmem_bytes(c) <= VMEM_CAP - VMEM_SLACK)

    def out_rows(k, m):
        return m if nk == 1 else jnp.where(k == last_k, m, 0)

    in_specs = [
        pl.BlockSpec((bm, bk), lambda n, k, m: (m, k)),
        pl.BlockSpec((None, bk, bn), lambda n, k, m: (layer, k, n)),
    ]
    args = [x, w]
    if epilogue == "gate_res":
        in_specs.append(pl.BlockSpec((bm, bn), lambda n, k, m: (out_rows(k, m), n)))
        in_specs.append(pl.BlockSpec((vec.shape[0], bn), lambda n, k, m: (0, vec_col * (N // bn) + n)))
        args += [res, vec]
    elif epilogue == "bias":
        in_specs.append(pl.BlockSpec((None, 1, bn), lambda n, k, m: (layer, 0, n)))
        args.append(vec)
    norm_tiles = None
    if epilogue == "head_norm":
        in_specs.append(pl.BlockSpec(vec.shape, lambda n, k, m: (0, 0)))
        args.append(vec)
        assert norm_cols[0] % bn == 0 and norm_cols[1] % bn == 0 and bn % HEAD_DIM == 0
        norm_tiles = (norm_cols[0] // bn, norm_cols[1] // bn)
    scratch = [pltpu.VMEM((bk, bn), BF16)]
    if nk > 1:
        scratch.append(pltpu.VMEM((M, bn), F32))
    body = functools.partial(_mm_body, nk=nk, bm=bm, epilogue=epilogue, prologue=prologue, n_x=n_x,
                             norm_tiles=norm_tiles)
    return pl.pallas_call(
        body,
        grid=(N // bn, nk, M // bm),
        in_specs=in_specs,
        out_specs=pl.BlockSpec((bm, bn), lambda n, k, m: (out_rows(k, m), n)),
        out_shape=jax.ShapeDtypeStruct((M, N), out_dtype),
        scratch_shapes=scratch,
        compiler_params=pltpu.CompilerParams(
            dimension_semantics=("arbitrary", "arbitrary", "arbitrary"),
            vmem_limit_bytes=_vmem_limit(vmem_bytes(bm))),
        name=name,
    )(*args)


def _norm_body(x_ref, g_ref, sh_ref, sc_ref, o_ref, *, nx_tiles):
    t = (pl.program_id(0) >= nx_tiles).astype(jnp.int32)
    x = x_ref[...]
    y = x * lax.rsqrt(jnp.mean(x * x, axis=-1, keepdims=True) + EPS)
    y = y * g_ref[...]
    shift = sh_ref[pl.ds(t, 1), :]
    scale = sc_ref[pl.ds(t, 1), :]
    o_ref[...] = (y * (1.0 + scale) + shift).astype(o_ref.dtype)


def _norm_modulate(x, gain, mods, *, shift_col, scale_col, n_x, m_rows=None):
    M = x.shape[0] if m_rows is None else m_rows
    D = x.shape[1]
    n_ctx = x.shape[0] - n_x
    bm = _largest_divisor(int(np.gcd(n_x, n_ctx)), (256, 128, 64, 32, 16))
    nbytes = 2 * bm * D * 4 + 2 * bm * D * 2 + 6 * 8 * D * 4
    return pl.pallas_call(
        functools.partial(_norm_body, nx_tiles=n_x // bm),
        grid=(M // bm,),
        in_specs=[
            pl.BlockSpec((bm, D), lambda i: (i, 0)),
            pl.BlockSpec((1, D), lambda i: (0, 0)),
            pl.BlockSpec((mods.shape[0], D), lambda i: (0, shift_col)),
            pl.BlockSpec((mods.shape[0], D), lambda i: (0, scale_col)),
        ],
        out_specs=pl.BlockSpec((bm, D), lambda i: (i, 0)),
        out_shape=jax.ShapeDtypeStruct((M, D), BF16),
        compiler_params=pltpu.CompilerParams(
            dimension_semantics=("arbitrary",), vmem_limit_bytes=_vmem_limit(nbytes)),
        name="norm_modulate",
    )(x, gain.reshape(1, D), mods, mods)


def _rms_gain(xf, gain):
    return (xf * lax.rsqrt(jnp.mean(xf * xf, axis=-1, keepdims=True) + EPS)) * gain


def _softmax_pv(parts, extra_logit=None):
    m_tile = None
    for s, _ in parts:
        for t in range(s.shape[1] // LANES):
            st = s[:, t * LANES:(t + 1) * LANES]
            m_tile = st if m_tile is None else jnp.maximum(m_tile, st)
    m = jnp.max(m_tile, axis=-1, keepdims=True)
    if extra_logit is not None:
        m = jnp.maximum(m, extra_logit)
    acc = None
    for s, v in parts:
        v_ones = jnp.concatenate([v, jnp.ones_like(v)], axis=1)
        ov = jnp.dot(jnp.exp2(s - m).astype(BF16), v_ones, preferred_element_type=F32)
        acc = ov if acc is None else acc + ov
    d = acc.shape[1] // 2
    o, l = acc[:, :d], acc[:, d:]
    if extra_logit is not None:
        l = l + jnp.exp2(extra_logit - m)
    return o / l


def _na_bias_planes(rb_ref, tb_ref, kh):
    W, kw = GRID_W, NA_KW
    shape = (W, 2 * W)
    qc = lax.broadcasted_iota(jnp.int32, shape, 0)
    lane = lax.broadcasted_iota(jnp.int32, shape, 1)
    kc = lane % W
    win = jnp.clip(qc - kw // 2, 0, W - kw)
    in_window = (kc >= win) & (kc < win + kw)
    left = lane < W
    neg = jnp.full(shape, NEG_INF, F32)

    def toeplitz(d, lane0):
        if d < 0 or d > 2 * kh - 2:
            return neg
        row = jnp.broadcast_to(rb_ref[0, d:d + 1, :], shape)
        return pltpu.roll(row, (lane0 - (kw - 1)) % LANES, 1, stride=1, stride_axis=0)

    for p in range(2 * kh + 1):
        both = jnp.where(in_window, jnp.where(left, toeplitz(p - 1, 0), toeplitz(p, W)), NEG_INF)
        tb_ref[0, p] = both
        tb_ref[1, p] = jnp.where(left, both, NEG_INF)
        tb_ref[2, p] = jnp.where(left, NEG_INF, both)


def _na_body(q_ref, k_ref, v_ref, rb_ref, o_ref, tb_ref, sl_ref, sc_ref, *, n_x, n_ctx, rows, kh):
    W = GRID_W
    nkr = kh + 2
    n_steps = rows // 2
    kc = k_ref[n_x:n_x + n_ctx, :]
    vc = v_ref[n_x:n_x + n_ctx, :]
    _na_bias_planes(rb_ref, tb_ref, kh)

    def windows(i):
        r = 2 * i
        a = jnp.clip(r - kh // 2, 0, rows - nkr)
        return (r, a, pl.ds(pl.multiple_of(r * W, 2 * W), 2 * W),
                pl.ds(pl.multiple_of(a * W, 2 * W), nkr * W))

    def scores(i, slot):
        r, a, qrows, krows = windows(i)
        q2 = q_ref[qrows, :]
        s_loc = lax.dot_general(q2, k_ref[krows, :], _NT, preferred_element_type=F32)
        bias_rows = []
        for j in range(2):
            qr = r + j
            r0 = jnp.clip(qr - kh // 2, 0, rows - kh)
            pieces = []
            for t in range(nkr // 2):
                kr0 = a + 2 * t
                in0 = (kr0 >= r0) & (kr0 < r0 + kh)
                in1 = (kr0 + 1 >= r0) & (kr0 + 1 < r0 + kh)
                variant = jnp.where(in0 & in1, 0, jnp.where(in0, 1, 2))
                plane = jnp.where(in0 | in1, jnp.clip(kr0 - qr + kh, 0, 2 * kh - 1), 2 * kh)
                pieces.append(tb_ref[variant, plane])
            bias_rows.append(jnp.concatenate(pieces, axis=1))
        sl_ref[slot] = s_loc + jnp.concatenate(bias_rows, axis=0)
        sc_ref[slot] = lax.dot_general(q2, kc, _NT, preferred_element_type=F32)

    def attend(i, slot):
        _, _, qrows, krows = windows(i)
        o = _softmax_pv([(sl_ref[slot], v_ref[krows, :]), (sc_ref[slot], vc)])
        o_ref[qrows, :] = o.astype(o_ref.dtype)

    scores(0, 0)
    scores(1, 1)

    def pairs(u, carry):
        for ph in range(PAIRS_PER_BODY):
            t = 2 * (PAIRS_PER_BODY * u + ph)
            cur = 2 * (ph % 2)
            for j in range(2):
                scores(jnp.minimum(t + 2 + j, n_steps - 1), 2 - cur + j)
                attend(t + j, cur + j)
        return carry

    lax.fori_loop(0, n_steps // (2 * PAIRS_PER_BODY), pairs, 0)

    ctx_rows = slice(n_x, n_x + n_ctx)
    s = lax.dot_general(q_ref[ctx_rows, :], kc, _NT, preferred_element_type=F32)
    o_ref[ctx_rows, :] = _softmax_pv([(s, vc)]).astype(o_ref.dtype)

def _na_attention(qkv, rel_bias, *, n_x, n_ctx):
    mtot, width = qkv.shape
    H = width // (3 * HEAD_DIM)
    rows = n_x // GRID_W
    kh = min(NA_KH, rows)
    assert GRID_W * 2 == LANES and rows % (4 * PAIRS_PER_BODY) == 0 and rows >= kh + 2
    assert kh == NA_KH and kh % 2 == 0 and PAIRS_PER_BODY % 2 == 0 and SCORE_SLOTS == 4
    assert rel_bias.shape == (H, 2 * kh - 1, 2 * NA_KW - 1) and 2 * NA_KW - 1 <= LANES
    rb = jnp.pad(rel_bias.astype(F32) * LOG2E, ((0, 0), (0, 1), (0, LANES - (2 * NA_KW - 1))))
    blk = (mtot, HEAD_DIM)
    nq, nk = 2 * GRID_W, (kh + 2) * GRID_W
    plane_shape = (3, 2 * kh + 1, GRID_W, 2 * GRID_W)
    nbytes = (8 * mtot * HEAD_DIM * 2 + int(np.prod(plane_shape)) * 4
              + SCORE_SLOTS * nq * (nk + n_ctx) * 4)
    return pl.pallas_call(
        functools.partial(_na_body, n_x=n_x, n_ctx=n_ctx, rows=rows, kh=kh),
        grid=(H,),
        in_specs=[
            pl.BlockSpec(blk, lambda h: (0, h)),
            pl.BlockSpec(blk, lambda h: (0, H + h)),
            pl.BlockSpec(blk, lambda h: (0, 2 * H + h)),
            pl.BlockSpec((1,) + rb.shape[1:], lambda h: (h, 0, 0)),
        ],
        out_specs=pl.BlockSpec(blk, lambda h: (0, h)),
        out_shape=jax.ShapeDtypeStruct((mtot, H * HEAD_DIM), BF16),
        scratch_shapes=[pltpu.VMEM(plane_shape, F32), pltpu.VMEM((SCORE_SLOTS, nq, nk), F32),
                        pltpu.VMEM((SCORE_SLOTS, nq, n_ctx), F32)],
        compiler_params=pltpu.CompilerParams(
            dimension_semantics=("arbitrary",), vmem_limit_bytes=_vmem_limit(nbytes)),
        name="na_attention",
    )(qkv, qkv, qkv, rb)


def _wa_body(q_ref, k_ref, v_ref, ca_ref, sa_ref, cb_ref, sb_ref, band_ref, sink_ref,
             o_ref, kn_ref, sl_ref, sc_ref, *, n_x, n_ctx, group):
    W = WA_WINDOW
    rpb = W // GRID_W
    nblk = n_x // W
    g = pl.program_id(1)
    lane = lax.broadcasted_iota(jnp.int32, (W, HEAD_DIM), 1)
    neg_half = (lane % (HEAD_DIM // 2)) < HEAD_DIM // 4

    def rope(y, b):
        def table(row_ref, col_ref):
            row_part = [jnp.broadcast_to(row_ref[pl.ds(rpb * b + j, 1), :], (GRID_W, HEAD_DIM))
                        for j in range(rpb)]
            return jnp.concatenate(row_part, axis=0) + jnp.concatenate([col_ref[...]] * rpb, axis=0)
        rot = jnp.where(neg_half, -pltpu.roll(y, HEAD_DIM - HEAD_DIM // 4, 1),
                        pltpu.roll(y, HEAD_DIM // 4, 1))
        return y * table(ca_ref, cb_ref) + rot * table(sa_ref, sb_ref)

    @pl.when(g == 0)
    def _prepare_keys():
        def kblock(b, carry):
            r = pl.ds(pl.multiple_of(b * W, W), W)
            kn_ref[r, :] = rope(k_ref[r, :].astype(F32), b).astype(BF16)
            return carry
        lax.fori_loop(0, nblk, kblock, 0, unroll=2)
        kn_ref[n_x:n_x + n_ctx, :] = k_ref[n_x:n_x + n_ctx, :]

    sink = sink_ref[pl.program_id(0) * group + g] * LOG2E
    kc = kn_ref[n_x:n_x + n_ctx, :]
    vc = v_ref[n_x:n_x + n_ctx, :]

    def windows(b):
        a = jnp.clip((b - 1) * W, 0, n_x - 3 * W)
        return pl.ds(pl.multiple_of(b * W, W), W), pl.ds(pl.multiple_of(a, W), 3 * W)

    def scores(b, slot):
        r, krows = windows(b)
        qn = rope(q_ref[r, :].astype(F32), b).astype(BF16)
        s_loc = lax.dot_general(qn, kn_ref[krows, :], _NT, preferred_element_type=F32)
        edge = jnp.where(b == 0, 0, jnp.where(b == nblk - 1, 2, 1))
        sl_ref[slot] = s_loc + band_ref[edge]
        sc_ref[slot] = lax.dot_general(qn, kc, _NT, preferred_element_type=F32)

    def attend(b, slot):
        r, krows = windows(b)
        o = _softmax_pv([(sl_ref[slot], v_ref[krows, :]), (sc_ref[slot], vc)], extra_logit=sink)
        o_ref[r, :] = o.astype(o_ref.dtype)

    scores(0, 0)
    scores(1, 1)

    def pairs(u, carry):
        for ph in range(PAIRS_PER_BODY):
            t = 2 * (PAIRS_PER_BODY * u + ph)
            cur = 2 * (ph % 2)
            for j in range(2):
                scores(jnp.minimum(t + 2 + j, nblk - 1), 2 - cur + j)
                attend(t + j, cur + j)
        return carry

    lax.fori_loop(0, nblk // (2 * PAIRS_PER_BODY), pairs, 0)

    for c in range(n_ctx // W):
        r = slice(n_x + c * W, n_x + (c + 1) * W)
        s = lax.dot_general(q_ref[r, :], kc, _NT, preferred_element_type=F32)
        o_ref[r, :] = _softmax_pv([(s, vc)], extra_logit=sink).astype(o_ref.dtype)


def _rope_tables(rows):
    axis_dim = HEAD_DIM // 2
    inv = ROPE_BASE ** (-jnp.arange(0, axis_dim, 2, dtype=F32) / axis_dim)
    ar = jnp.arange(rows, dtype=F32)[:, None] * inv
    ac = jnp.arange(GRID_W, dtype=F32)[:, None] * inv
    zr = jnp.zeros((rows, axis_dim), F32)
    zc = jnp.zeros((GRID_W, axis_dim), F32)

    def row_tab(f):
        return jnp.concatenate([f(ar), f(ar), zr], axis=-1)

    def col_tab(f):
        return jnp.concatenate([zc, f(ac), f(ac)], axis=-1)

    return row_tab(jnp.cos), row_tab(jnp.sin), col_tab(jnp.cos), col_tab(jnp.sin)


def _wa_band_bias():
    W = WA_WINDOW
    dist = np.arange(3 * W)[None, :] - np.arange(W)[:, None]
    return np.stack([np.where(np.abs(dist + d) <= W, 0.0, NEG_INF) for d in (0, -W, -2 * W)]
                    ).astype(np.float32)


def _wa_attention(qkv, sink, *, n_x, n_ctx, n_heads):
    mtot = qkv.shape[0]
    H, KVH = n_heads, WA_KV_HEADS
    G = H // KVH
    W = WA_WINDOW
    assert W % GRID_W == 0 and n_x % (2 * PAIRS_PER_BODY * W) == 0 and n_ctx % W == 0
    assert W == HEAD_DIM and PAIRS_PER_BODY % 2 == 0 and SCORE_SLOTS == 4
    ca, sa, cb, sb = _rope_tables(n_x // GRID_W)
    band = jnp.asarray(_wa_band_bias())
    blk = (mtot, HEAD_DIM)
    full = lambda arr: pl.BlockSpec(arr.shape, lambda kv, g: (0,) * arr.ndim)
    nbytes = (8 * mtot * HEAD_DIM * 2 + mtot * HEAD_DIM * 2 + 4 * (ca.size + cb.size) * 4
              + 2 * band.size * 4 + SCORE_SLOTS * W * (3 * W + n_ctx) * 4)
    return pl.pallas_call(
        functools.partial(_wa_body, n_x=n_x, n_ctx=n_ctx, group=G),
        grid=(KVH, G),
        in_specs=[
            pl.BlockSpec(blk, lambda kv, g: (0, kv * G + g)),
            pl.BlockSpec(blk, lambda kv, g: (0, H + kv)),
            pl.BlockSpec(blk, lambda kv, g: (0, H + KVH + kv)),
            full(ca), full(sa), full(cb), full(sb), full(band),
            pl.BlockSpec(memory_space=pltpu.SMEM),
        ],
        out_specs=pl.BlockSpec(blk, lambda kv, g: (0, kv * G + g)),
        out_shape=jax.ShapeDtypeStruct((mtot, H * HEAD_DIM), BF16),
        scratch_shapes=[pltpu.VMEM(blk, BF16), pltpu.VMEM((SCORE_SLOTS, W, 3 * W), F32),
                        pltpu.VMEM((SCORE_SLOTS, W, n_ctx), F32)],
        compiler_params=pltpu.CompilerParams(
            dimension_semantics=("arbitrary", "arbitrary"), vmem_limit_bytes=_vmem_limit(nbytes)),
        name="wa_attention",
    )(qkv, qkv, qkv, ca, sa, cb, sb, band, sink.astype(F32))


def _sg_body(u_ref, v_ref, vg_ref, ws_ref, bs_ref, o_ref, *, n_chunks, groups):
    P = SG_CHUNK
    gw = v_ref.shape[1] // groups
    vn = _rms_gain(v_ref[...].astype(F32), vg_ref[...]).astype(BF16)
    for g in range(groups):
        wg = ws_ref[g].astype(BF16)
        cols = slice(g * gw, (g + 1) * gw)
        for c in range(n_chunks):
            rows = slice(c * P, (c + 1) * P)
            sv = jnp.dot(wg, vn[rows, cols], preferred_element_type=F32) + bs_ref[g]
            o_ref[rows, cols] = (u_ref[rows, cols].astype(F32) * sv).astype(o_ref.dtype)


def _spatial_gate(uv, v_gain, w_s, b_s):
    mtot, two_d = uv.shape
    D = two_d // 2
    G, P = SG_GROUPS, SG_CHUNK
    gw = D // G
    assert gw % LANES == 0 and mtot % P == 0
    n_chunks = _largest_divisor(mtot // P, (3, 2, 1))
    bm = n_chunks * P
    bias = jnp.broadcast_to(b_s.astype(F32)[:, :, None], (G, P, gw))
    nbytes = 6 * bm * D * 2 + bm * D * 6 + 2 * (w_s.size + bias.size) * 4
    return pl.pallas_call(
        functools.partial(_sg_body, n_chunks=n_chunks, groups=G),
        grid=(mtot // bm,),
        in_specs=[
            pl.BlockSpec((bm, D), lambda i: (i, 0)),
            pl.BlockSpec((bm, D), lambda i: (i, 1)),
            pl.BlockSpec((1, D), lambda i: (0, 0)),
            pl.BlockSpec(w_s.shape, lambda i: (0, 0, 0)),
            pl.BlockSpec(bias.shape, lambda i: (0, 0, 0)),
        ],
        out_specs=pl.BlockSpec((bm, D), lambda i: (i, 0)),
        out_shape=jax.ShapeDtypeStruct((mtot, D), BF16),
        compiler_params=pltpu.CompilerParams(
            dimension_semantics=("arbitrary",), vmem_limit_bytes=_vmem_limit(nbytes)),
        name="spatial_gate",
    )(uv, uv, v_gain.reshape(1, D), w_s, bias)


def _head_gains(q_gain, k_gain):
    scale = np.float32(HEAD_DIM ** -0.5) * LOG2E
    rows = jnp.stack([q_gain.astype(F32) * scale, k_gain.astype(F32)])
    return jnp.concatenate([rows, jnp.zeros((6, HEAD_DIM), F32)], axis=0)


def _ada_modulation(cond, a, b, bias, layer):
    t = _matmul(cond, a, layer, out_dtype=F32, prologue="silu", name="ada_down")
    return _matmul(t, b, layer, out_dtype=F32, epilogue="bias", vec=bias[:, None, :],
                   bn=_largest_divisor(b.shape[2], (2048, 1024, 512, 256, 128)), name="ada_up")


def kernel(x, c, ctx, c_ctx, ada_a, ada_b, ada_bias, norm_mix, norm_mlp, mlp_w1, mlp_w2,
           na_w_qkv, na_w_o, na_q_norm, na_k_norm, na_rel_bias,
           wa_w_qkv, wa_w_o, wa_q_norm, wa_k_norm, wa_sink,
           sg_w_in, sg_v_norm, sg_w_s, sg_b_s, sg_w_out):
    assert x.shape[0] == 1 and ctx.shape[0] == 1 and c.shape[0] == 1
    n_x, D = x.shape[1], x.shape[2]
    n_ctx = ctx.shape[1]
    depth = ada_a.shape[0]
    n_heads = D // HEAD_DIM
    n_mixers = 3

    xc = jnp.concatenate([x[0], ctx[0]], axis=0)
    cond = jnp.zeros((16, D), F32).at[0].set(c[0]).at[1].set(c_ctx)

    for i in range(depth):
        last = i == depth - 1
        kind, j = i % n_mixers, i // n_mixers
        mods = _ada_modulation(cond, ada_a, ada_b, ada_bias, i)
        gate = dict(epilogue="gate_res", vec=mods, n_x=n_x, out_dtype=F32,
                    m_rows=n_x if last else None)

        h = _norm_modulate(xc, norm_mix[i], mods, shift_col=0, scale_col=1, n_x=n_x)
        if kind == 0:
            qkv = _matmul(h, na_w_qkv, j, out_dtype=BF16, epilogue="head_norm",
                          vec=_head_gains(na_q_norm[j], na_k_norm[j]), norm_cols=(D, 2 * D),
                          name="na_qkv")
            o = _na_attention(qkv, na_rel_bias[j], n_x=n_x, n_ctx=n_ctx)
            xc = _matmul(o, na_w_o, j, res=xc, vec_col=2, name="na_out", **gate)
        elif kind == 1:
            qkv = _matmul(h, wa_w_qkv, j, out_dtype=BF16, epilogue="head_norm",
                          vec=_head_gains(wa_q_norm[j], wa_k_norm[j]),
                          norm_cols=(D, D + WA_KV_HEADS * HEAD_DIM), name="wa_qkv")
            o = _wa_attention(qkv, wa_sink[j], n_x=n_x, n_ctx=n_ctx, n_heads=n_heads)
            xc = _matmul(o, wa_w_o, j, res=xc, vec_col=2, name="wa_out", **gate)
        else:
            uv = _matmul(h, sg_w_in, j, out_dtype=BF16, epilogue="gelu", name="sg_in")
            z = _spatial_gate(uv, sg_v_norm[j], sg_w_s[j], sg_b_s[j])
            xc = _matmul(z, sg_w_out, j, res=xc, vec_col=2, name="sg_out", **gate)

        h2 = _norm_modulate(xc, norm_mlp[i], mods, shift_col=3, scale_col=4, n_x=n_x)
        hid = _matmul(h2, mlp_w1, i, out_dtype=BF16, epilogue="relu2", name="mlp_up")
        xc = _matmul(hid, mlp_w2, i, res=xc, vec_col=5, name="mlp_down", **gate)
    return xc[None]
```

```python
import functools

import numpy as np
import jax
import jax.numpy as jnp
from jax import lax
from jax.experimental import pallas as pl
from jax.experimental.pallas import tpu as pltpu

GRID_W = 64
HEAD_DIM = 128
NA_KH = 8
NA_KW = 16
WA_KV_HEADS = 8
WA_WINDOW = 128
ROPE_BASE = 10000.0
SG_GROUPS = 8
SG_CHUNK = 128
N_MOD = 6
EPS = 1e-6
NEG_INF = -1e30
LOG2E = np.float32(np.log2(np.e))
PAIRS_PER_BODY = 8
ROW_CHUNKS = 4
SCORE_SLOTS = 4

LANES = 128
V7X_VMEM_BYTES = 64 * 1024 * 1024
VMEM_CAP = V7X_VMEM_BYTES - 6 * 1024 * 1024
VMEM_SLACK = 2 * 1024 * 1024

F32 = jnp.float32
BF16 = jnp.bfloat16
_NT = (((1,), (1,)), ((), ()))


def _vmem_limit(nbytes):
    return int(min(VMEM_CAP, nbytes + 4 * VMEM_SLACK))


def _largest_divisor(n, candidates):
    for c in candidates:
        if n % c == 0:
            return c
    raise ValueError(f"no tile in {candidates} divides {n}")


def _mm_body(*refs, nk, bm, epilogue, prologue, n_x, norm_tiles, row_chunks):
    x_ref, w_ref = refs[0], refs[1]
    pos = 2
    res_ref = vec_ref = None
    if epilogue == "gate_res":
        res_ref, vec_ref = refs[2], refs[3]
        pos = 4
    elif epilogue in ("bias", "head_norm"):
        vec_ref = refs[2]
        pos = 3
    o_ref, wbf_ref = refs[pos], refs[pos + 1]
    acc_ref = refs[pos + 2] if nk > 1 else None
    k = pl.program_id(1)
    m = pl.program_id(2)

    @pl.when(m == 0)
    def _cast_weights():
        wbf_ref[...] = w_ref[...].astype(BF16)

    if nk > 1:
        rows = pl.ds(pl.multiple_of(m * bm, bm), bm)

        @pl.when(k == 0)
        def _zero():
            acc_ref[rows, :] = jnp.zeros((bm, acc_ref.shape[1]), F32)

    def matmul_rows(r0, nr):
        xv = x_ref[r0:r0 + nr, :]
        if prologue == "silu":
            xv = xv * jax.nn.sigmoid(xv)
        return jnp.dot(xv.astype(BF16), wbf_ref[...], preferred_element_type=F32)

    def finish(acc, r0, nr):
        out_rows = slice(r0, r0 + nr)
        if epilogue == "relu2":
            y = jnp.square(jnp.maximum(acc, 0.0))
        elif epilogue == "gelu":
            y = 0.5 * acc * (1.0 + lax.erf(acc * np.float32(np.sqrt(0.5))))
        elif epilogue == "gate_res":
            row_id = m * bm + r0 + lax.broadcasted_iota(jnp.int32, (nr, 1), 0)
            gate = jnp.where(row_id >= n_x, vec_ref[1:2, :], vec_ref[0:1, :])
            y = res_ref[out_rows, :] + gate * acc
        elif epilogue == "bias":
            y = acc + vec_ref[...]
        elif epilogue == "head_norm":
            n = pl.program_id(0)
            gain = vec_ref[pl.ds((n >= norm_tiles[0]).astype(jnp.int32), 1), :]
            for t in range(acc.shape[1] // HEAD_DIM):
                cols = slice(t * HEAD_DIM, (t + 1) * HEAD_DIM)
                a = acc[:, cols]
                o_ref[out_rows, cols] = jnp.where(n < norm_tiles[1], _rms_gain(a, gain), a
                                                  ).astype(o_ref.dtype)
            return
        else:
            y = acc
        o_ref[out_rows, :] = y.astype(o_ref.dtype)

    nr = bm // row_chunks
    if nk == 1:
        for c in range(row_chunks):
            finish(matmul_rows(c * nr, nr), c * nr, nr)
    else:
        for c in range(row_chunks):
            chunk = pl.ds(pl.multiple_of(m * bm + c * nr, nr), nr)
            acc_ref[chunk, :] = acc_ref[chunk, :] + matmul_rows(c * nr, nr)

        @pl.when(k == nk - 1)
        def _emit():
            finish(acc_ref[rows, :], 0, bm)


def _matmul(x, w, layer, *, out_dtype, epilogue=None, prologue=None, res=None, vec=None, vec_col=0,
            n_x=0, m_rows=None, norm_cols=None, bm=None, bn=None, bk=None, name="mm"):
    M = x.shape[0] if m_rows is None else m_rows
    _, K, N = w.shape
    bn = bn or _largest_divisor(int(np.gcd.reduce((N,) + tuple(norm_cols or ()))), (512, 256, 128))
    bk = bk or _largest_divisor(K, (4096, 2048, 1024, 512, 256, 128))
    nk = K // bk
    last_k = nk - 1
    out_bytes = jnp.dtype(out_dtype).itemsize

    def vmem_bytes(bm_):
        n = 2 * bm_ * bk * x.dtype.itemsize + 2 * bk * bn * 4 + bk * bn * 2 + 2 * bm_ * bn * out_bytes
        if epilogue == "gate_res":
            n += 2 * bm_ * bn * 4 + 2 * vec.shape[0] * bn * 4
        if nk > 1:
            n += M * bn * 4
        return n

    if bm is None:
        bm = next(c for c in (1408, 1024, 768, 512, 384, 256, 128, 16)
                  if M % c == 0 and vmem_bytes(c) <= VMEM_CAP - VMEM_SLACK)

    def out_rows(k, m):
        return m if nk == 1 else jnp.where(k == last_k, m, 0)

    in_specs = [
        pl.BlockSpec((bm, bk), lambda n, k, m: (m, k)),
        pl.BlockSpec((None, bk, bn), lambda n, k, m: (layer, k, n)),
    ]
    args = [x, w]
    if epilogue == "gate_res":
        in_specs.append(pl.BlockSpec((bm, bn), lambda n, k, m: (out_rows(k, m), n)))
        in_specs.append(pl.BlockSpec((vec.shape[0], bn), lambda n, k, m: (0, vec_col * (N // bn) + n)))
        args += [res, vec]
    elif epilogue == "bias":
        in_specs.append(pl.BlockSpec((None, 1, bn), lambda n, k, m: (layer, 0, n)))
        args.append(vec)
    norm_tiles = None
    if epilogue == "head_norm":
        in_specs.append(pl.BlockSpec(vec.shape, lambda n, k, m: (0, 0)))
        args.append(vec)
        assert norm_cols[0] % bn == 0 and norm_cols[1] % bn == 0 and bn % HEAD_DIM == 0
        norm_tiles = (norm_cols[0] // bn, norm_cols[1] // bn)
    scratch = [pltpu.VMEM((bk, bn), BF16)]
    if nk > 1:
        scratch.append(pltpu.VMEM((M, bn), F32))
    row_chunks = next(c for c in (ROW_CHUNKS, 2, 1) if bm % (16 * c) == 0)
    body = functools.partial(_mm_body, nk=nk, bm=bm, epilogue=epilogue, prologue=prologue, n_x=n_x,
                             norm_tiles=norm_tiles, row_chunks=row_chunks)
    return pl.pallas_call(
        body,
        grid=(N // bn, nk, M // bm),
        in_specs=in_specs,
        out_specs=pl.BlockSpec((bm, bn), lambda n, k, m: (out_rows(k, m), n)),
        out_shape=jax.ShapeDtypeStruct((M, N), out_dtype),
        scratch_shapes=scratch,
        compiler_params=pltpu.CompilerParams(
            dimension_semantics=("arbitrary", "arbitrary", "arbitrary"),
            vmem_limit_bytes=_vmem_limit(vmem_bytes(bm))),
        name=name,
    )(*args)


def _norm_body(x_ref, g_ref, sh_ref, sc_ref, o_ref, *, nx_tiles):
    t = (pl.program_id(0) >= nx_tiles).astype(jnp.int32)
    x = x_ref[...]
    y = x * lax.rsqrt(jnp.mean(x * x, axis=-1, keepdims=True) + EPS)
    y = y * g_ref[...]
    shift = sh_ref[pl.ds(t, 1), :]
    scale = sc_ref[pl.ds(t, 1), :]
    o_ref[...] = (y * (1.0 + scale) + shift).astype(o_ref.dtype)


def _norm_modulate(x, gain, mods, *, shift_col, scale_col, n_x, m_rows=None):
    M = x.shape[0] if m_rows is None else m_rows
    D = x.shape[1]
    n_ctx = x.shape[0] - n_x
    bm = _largest_divisor(int(np.gcd(n_x, n_ctx)), (256, 128, 64, 32, 16))
    nbytes = 2 * bm * D * 4 + 2 * bm * D * 2 + 6 * 8 * D * 4
    return pl.pallas_call(
        functools.partial(_norm_body, nx_tiles=n_x // bm),
        grid=(M // bm,),
        in_specs=[
            pl.BlockSpec((bm, D), lambda i: (i, 0)),
            pl.BlockSpec((1, D), lambda i: (0, 0)),
            pl.BlockSpec((mods.shape[0], D), lambda i: (0, shift_col)),
            pl.BlockSpec((mods.shape[0], D), lambda i: (0, scale_col)),
        ],
        out_specs=pl.BlockSpec((bm, D), lambda i: (i, 0)),
        out_shape=jax.ShapeDtypeStruct((M, D), BF16),
        compiler_params=pltpu.CompilerParams(
            dimension_semantics=("arbitrary",), vmem_limit_bytes=_vmem_limit(nbytes)),
        name="norm_modulate",
    )(x, gain.reshape(1, D), mods, mods)


def _rms_gain(xf, gain):
    return (xf * lax.rsqrt(jnp.mean(xf * xf, axis=-1, keepdims=True) + EPS)) * gain


def _softmax_pv(parts, extra_logit=None):
    m_tile = None
    for s, _ in parts:
        for t in range(s.shape[1] // LANES):
            st = s[:, t * LANES:(t + 1) * LANES]
            m_tile = st if m_tile is None else jnp.maximum(m_tile, st)
    m = jnp.max(m_tile, axis=-1, keepdims=True)
    if extra_logit is not None:
        m = jnp.maximum(m, extra_logit)
    acc = None
    for s, v in parts:
        v_ones = jnp.concatenate([v, jnp.ones_like(v)], axis=1)
        ov = jnp.dot(jnp.exp2(s - m).astype(BF16), v_ones, preferred_element_type=F32)
        acc = ov if acc is None else acc + ov
    d = acc.shape[1] // 2
    o, l = acc[:, :d], acc[:, d:]
    if extra_logit is not None:
        l = l + jnp.exp2(extra_logit - m)
    return o / l


def _na_bias_planes(rb_ref, tb_ref, kh):
    W, kw = GRID_W, NA_KW
    shape = (W, 2 * W)
    qc = lax.broadcasted_iota(jnp.int32, shape, 0)
    lane = lax.broadcasted_iota(jnp.int32, shape, 1)
    kc = lane % W
    win = jnp.clip(qc - kw // 2, 0, W - kw)
    in_window = (kc >= win) & (kc < win + kw)
    left = lane < W
    neg = jnp.full(shape, NEG_INF, F32)

    def toeplitz(d, lane0):
        if d < 0 or d > 2 * kh - 2:
            return neg
        row = jnp.broadcast_to(rb_ref[0, d:d + 1, :], shape)
        return pltpu.roll(row, (lane0 - (kw - 1)) % LANES, 1, stride=1, stride_axis=0)

    for p in range(2 * kh + 1):
        both = jnp.where(in_window, jnp.where(left, toeplitz(p - 1, 0), toeplitz(p, W)), NEG_INF)
        tb_ref[0, p] = both
        tb_ref[1, p] = jnp.where(left, both, NEG_INF)
        tb_ref[2, p] = jnp.where(left, NEG_INF, both)


def _na_body(q_ref, k_ref, v_ref, rb_ref, o_ref, tb_ref, sl_ref, sc_ref, *, n_x, n_ctx, rows, kh):
    W = GRID_W
    nkr = kh + 2
    n_steps = rows // 2
    kc = k_ref[n_x:n_x + n_ctx, :]
    vc = v_ref[n_x:n_x + n_ctx, :]
    _na_bias_planes(rb_ref, tb_ref, kh)

    def windows(i):
        r = 2 * i
        a = jnp.clip(r - kh // 2, 0, rows - nkr)
        return (r, a, pl.ds(pl.multiple_of(r * W, 2 * W), 2 * W),
                pl.ds(pl.multiple_of(a * W, 2 * W), nkr * W))

    def scores(i, slot):
        r, a, qrows, krows = windows(i)
        q2 = q_ref[qrows, :]
        s_loc = lax.dot_general(q2, k_ref[krows, :], _NT, preferred_element_type=F32)
        bias_rows = []
        for j in range(2):
            qr = r + j
            r0 = jnp.clip(qr - kh // 2, 0, rows - kh)
            pieces = []
            for t in range(nkr // 2):
                kr0 = a + 2 * t
                in0 = (kr0 >= r0) & (kr0 < r0 + kh)
                in1 = (kr0 + 1 >= r0) & (kr0 + 1 < r0 + kh)
                variant = jnp.where(in0 & in1, 0, jnp.where(in0, 1, 2))
                plane = jnp.where(in0 | in1, jnp.clip(kr0 - qr + kh, 0, 2 * kh - 1), 2 * kh)
                pieces.append(tb_ref[variant, plane])
            bias_rows.append(jnp.concatenate(pieces, axis=1))
        sl_ref[slot] = s_loc + jnp.concatenate(bias_rows, axis=0)
        sc_ref[slot] = lax.dot_general(q2, kc, _NT, preferred_element_type=F32)

    def attend(i, slot):
        _, _, qrows, krows = windows(i)
        o = _softmax_pv([(sl_ref[slot], v_ref[krows, :]), (sc_ref[slot], vc)])
        o_ref[qrows, :] = o.astype(o_ref.dtype)

    scores(0, 0)
    scores(1, 1)

    def pairs(u, carry):
        for ph in range(PAIRS_PER_BODY):
            t = 2 * (PAIRS_PER_BODY * u + ph)
            cur = 2 * (ph % 2)
            for j in range(2):
                scores(jnp.minimum(t + 2 + j, n_steps - 1), 2 - cur + j)
                attend(t + j, cur + j)
        return carry

    lax.fori_loop(0, n_steps // (2 * PAIRS_PER_BODY), pairs, 0)

    ctx_rows = slice(n_x, n_x + n_ctx)
    s = lax.dot_general(q_ref[ctx_rows, :], kc, _NT, preferred_element_type=F32)
    o_ref[ctx_rows, :] = _softmax_pv([(s, vc)]).astype(o_ref.dtype)

def _na_attention(qkv, rel_bias, *, n_x, n_ctx):
    mtot, width = qkv.shape
    H = width // (3 * HEAD_DIM)
    rows = n_x // GRID_W
    kh = min(NA_KH, rows)
    assert GRID_W * 2 == LANES and rows % (4 * PAIRS_PER_BODY) == 0 and rows >= kh + 2
    assert kh == NA_KH and kh % 2 == 0 and PAIRS_PER_BODY % 2 == 0 and SCORE_SLOTS == 4
    assert rel_bias.shape == (H, 2 * kh - 1, 2 * NA_KW - 1) and 2 * NA_KW - 1 <= LANES
    rb = jnp.pad(rel_bias.astype(F32) * LOG2E, ((0, 0), (0, 1), (0, LANES - (2 * NA_KW - 1))))
    blk = (mtot, HEAD_DIM)
    nq, nk = 2 * GRID_W, (kh + 2) * GRID_W
    plane_shape = (3, 2 * kh + 1, GRID_W, 2 * GRID_W)
    nbytes = (8 * mtot * HEAD_DIM * 2 + int(np.prod(plane_shape)) * 4
              + SCORE_SLOTS * nq * (nk + n_ctx) * 4)
    return pl.pallas_call(
        functools.partial(_na_body, n_x=n_x, n_ctx=n_ctx, rows=rows, kh=kh),
        grid=(H,),
        in_specs=[
            pl.BlockSpec(blk, lambda h: (0, h)),
            pl.BlockSpec(blk, lambda h: (0, H + h)),
            pl.BlockSpec(blk, lambda h: (0, 2 * H + h)),
            pl.BlockSpec((1,) + rb.shape[1:], lambda h: (h, 0, 0)),
        ],
        out_specs=pl.BlockSpec(blk, lambda h: (0, h)),
        out_shape=jax.ShapeDtypeStruct((mtot, H * HEAD_DIM), BF16),
        scratch_shapes=[pltpu.VMEM(plane_shape, F32), pltpu.VMEM((SCORE_SLOTS, nq, nk), F32),
                        pltpu.VMEM((SCORE_SLOTS, nq, n_ctx), F32)],
        compiler_params=pltpu.CompilerParams(
            dimension_semantics=("arbitrary",), vmem_limit_bytes=_vmem_limit(nbytes)),
        name="na_attention",
    )(qkv, qkv, qkv, rb)


def _wa_body(q_ref, k_ref, v_ref, ca_ref, sa_ref, cb_ref, sb_ref, band_ref, sink_ref,
             o_ref, kn_ref, sl_ref, sc_ref, *, n_x, n_ctx, group):
    W = WA_WINDOW
    rpb = W // GRID_W
    nblk = n_x // W
    g = pl.program_id(1)
    lane = lax.broadcasted_iota(jnp.int32, (W, HEAD_DIM), 1)
    neg_half = (lane % (HEAD_DIM // 2)) < HEAD_DIM // 4

    def rope(y, b):
        def table(row_ref, col_ref):
            row_part = [jnp.broadcast_to(row_ref[pl.ds(rpb * b + j, 1), :], (GRID_W, HEAD_DIM))
                        for j in range(rpb)]
            return jnp.concatenate(row_part, axis=0) + jnp.concatenate([col_ref[...]] * rpb, axis=0)
        rot = jnp.where(neg_half, -pltpu.roll(y, HEAD_DIM - HEAD_DIM // 4, 1),
                        pltpu.roll(y, HEAD_DIM // 4, 1))
        return y * table(ca_ref, cb_ref) + rot * table(sa_ref, sb_ref)

    @pl.when(g == 0)
    def _prepare_keys():
        def kblock(b, carry):
            r = pl.ds(pl.multiple_of(b * W, W), W)
            kn_ref[r, :] = rope(k_ref[r, :].astype(F32), b).astype(BF16)
            return carry
        lax.fori_loop(0, nblk, kblock, 0, unroll=2)
        kn_ref[n_x:n_x + n_ctx, :] = k_ref[n_x:n_x + n_ctx, :]

    sink = sink_ref[pl.program_id(0) * group + g] * LOG2E
    kc = kn_ref[n_x:n_x + n_ctx, :]
    vc = v_ref[n_x:n_x + n_ctx, :]

    def windows(b):
        a = jnp.clip((b - 1) * W, 0, n_x - 3 * W)
        return pl.ds(pl.multiple_of(b * W, W), W), pl.ds(pl.multiple_of(a, W), 3 * W)

    def scores(b, slot):
        r, krows = windows(b)
        qn = rope(q_ref[r, :].astype(F32), b).astype(BF16)
        s_loc = lax.dot_general(qn, kn_ref[krows, :], _NT, preferred_element_type=F32)
        edge = jnp.where(b == 0, 0, jnp.where(b == nblk - 1, 2, 1))
        sl_ref[slot] = s_loc + band_ref[edge]
        sc_ref[slot] = lax.dot_general(qn, kc, _NT, preferred_element_type=F32)

    def attend(b, slot):
        r, krows = windows(b)
        o = _softmax_pv([(sl_ref[slot], v_ref[krows, :]), (sc_ref[slot], vc)], extra_logit=sink)
        o_ref[r, :] = o.astype(o_ref.dtype)

    scores(0, 0)
    scores(1, 1)

    def pairs(u, carry):
        for ph in range(PAIRS_PER_BODY):
            t = 2 * (PAIRS_PER_BODY * u + ph)
            cur = 2 * (ph % 2)
            for j in range(2):
                scores(jnp.minimum(t + 2 + j, nblk - 1), 2 - cur + j)
                attend(t + j, cur + j)
        return carry

    lax.fori_loop(0, nblk // (2 * PAIRS_PER_BODY), pairs, 0)

    for c in range(n_ctx // W):
        r = slice(n_x + c * W, n_x + (c + 1) * W)
        s = lax.dot_general(q_ref[r, :], kc, _NT, preferred_element_type=F32)
        o_ref[r, :] = _softmax_pv([(s, vc)], extra_logit=sink).astype(o_ref.dtype)


def _rope_tables(rows):
    axis_dim = HEAD_DIM // 2
    inv = ROPE_BASE ** (-jnp.arange(0, axis_dim, 2, dtype=F32) / axis_dim)
    ar = jnp.arange(rows, dtype=F32)[:, None] * inv
    ac = jnp.arange(GRID_W, dtype=F32)[:, None] * inv
    zr = jnp.zeros((rows, axis_dim), F32)
    zc = jnp.zeros((GRID_W, axis_dim), F32)

    def row_tab(f):
        return jnp.concatenate([f(ar), f(ar), zr], axis=-1)

    def col_tab(f):
        return jnp.concatenate([zc, f(ac), f(ac)], axis=-1)

    return row_tab(jnp.cos), row_tab(jnp.sin), col_tab(jnp.cos), col_tab(jnp.sin)


def _wa_band_bias():
    W = WA_WINDOW
    dist = np.arange(3 * W)[None, :] - np.arange(W)[:, None]
    return np.stack([np.where(np.abs(dist + d) <= W, 0.0, NEG_INF) for d in (0, -W, -2 * W)]
                    ).astype(np.float32)


def _wa_attention(qkv, sink, *, n_x, n_ctx, n_heads):
    mtot = qkv.shape[0]
    H, KVH = n_heads, WA_KV_HEADS
    G = H // KVH
    W = WA_WINDOW
    assert W % GRID_W == 0 and n_x % (2 * PAIRS_PER_BODY * W) == 0 and n_ctx % W == 0
    assert W == HEAD_DIM and PAIRS_PER_BODY % 2 == 0 and SCORE_SLOTS == 4
    ca, sa, cb, sb = _rope_tables(n_x // GRID_W)
    band = jnp.asarray(_wa_band_bias())
    blk = (mtot, HEAD_DIM)
    full = lambda arr: pl.BlockSpec(arr.shape, lambda kv, g: (0,) * arr.ndim)
    nbytes = (8 * mtot * HEAD_DIM * 2 + mtot * HEAD_DIM * 2 + 4 * (ca.size + cb.size) * 4
              + 2 * band.size * 4 + SCORE_SLOTS * W * (3 * W + n_ctx) * 4)
    return pl.pallas_call(
        functools.partial(_wa_body, n_x=n_x, n_ctx=n_ctx, group=G),
        grid=(KVH, G),
        in_specs=[
            pl.BlockSpec(blk, lambda kv, g: (0, kv * G + g)),
            pl.BlockSpec(blk, lambda kv, g: (0, H + kv)),
            pl.BlockSpec(blk, lambda kv, g: (0, H + KVH + kv)),
            full(ca), full(sa), full(cb), full(sb), full(band),
            pl.BlockSpec(memory_space=pltpu.SMEM),
        ],
        out_specs=pl.BlockSpec(blk, lambda kv, g: (0, kv * G + g)),
        out_shape=jax.ShapeDtypeStruct((mtot, H * HEAD_DIM), BF16),
        scratch_shapes=[pltpu.VMEM(blk, BF16), pltpu.VMEM((SCORE_SLOTS, W, 3 * W), F32),
                        pltpu.VMEM((SCORE_SLOTS, W, n_ctx), F32)],
        compiler_params=pltpu.CompilerParams(
            dimension_semantics=("arbitrary", "arbitrary"), vmem_limit_bytes=_vmem_limit(nbytes)),
        name="wa_attention",
    )(qkv, qkv, qkv, ca, sa, cb, sb, band, sink.astype(F32))


def _sg_body(u_ref, v_ref, vg_ref, ws_ref, bs_ref, o_ref, *, n_chunks, groups):
    P = SG_CHUNK
    gw = v_ref.shape[1] // groups
    vn = _rms_gain(v_ref[...].astype(F32), vg_ref[...]).astype(BF16)
    for g in range(groups):
        wg = ws_ref[g].astype(BF16)
        cols = slice(g * gw, (g + 1) * gw)
        for c in range(n_chunks):
            rows = slice(c * P, (c + 1) * P)
            sv = jnp.dot(wg, vn[rows, cols], preferred_element_type=F32) + bs_ref[g]
            o_ref[rows, cols] = (u_ref[rows, cols].astype(F32) * sv).astype(o_ref.dtype)


def _spatial_gate(uv, v_gain, w_s, b_s):
    mtot, two_d = uv.shape
    D = two_d // 2
    G, P = SG_GROUPS, SG_CHUNK
    gw = D // G
    assert gw % LANES == 0 and mtot % P == 0
    n_chunks = _largest_divisor(mtot // P, (3, 2, 1))
    bm = n_chunks * P
    bias = jnp.broadcast_to(b_s.astype(F32)[:, :, None], (G, P, gw))
    nbytes = 6 * bm * D * 2 + bm * D * 6 + 2 * (w_s.size + bias.size) * 4
    return pl.pallas_call(
        functools.partial(_sg_body, n_chunks=n_chunks, groups=G),
        grid=(mtot // bm,),
        in_specs=[
            pl.BlockSpec((bm, D), lambda i: (i, 0)),
            pl.BlockSpec((bm, D), lambda i: (i, 1)),
            pl.BlockSpec((1, D), lambda i: (0, 0)),
            pl.BlockSpec(w_s.shape, lambda i: (0, 0, 0)),
            pl.BlockSpec(bias.shape, lambda i: (0, 0, 0)),
        ],
        out_specs=pl.BlockSpec((bm, D), lambda i: (i, 0)),
        out_shape=jax.ShapeDtypeStruct((mtot, D), BF16),
        compiler_params=pltpu.CompilerParams(
            dimension_semantics=("arbitrary",), vmem_limit_bytes=_vmem_limit(nbytes)),
        name="spatial_gate",
    )(uv, uv, v_gain.reshape(1, D), w_s, bias)


def _head_gains(q_gain, k_gain):
    scale = np.float32(HEAD_DIM ** -0.5) * LOG2E
    rows = jnp.stack([q_gain.astype(F32) * scale, k_gain.astype(F32)])
    return jnp.concatenate([rows, jnp.zeros((6, HEAD_DIM), F32)], axis=0)


def _ada_modulation(cond, a, b, bias, layer):
    t = _matmul(cond, a, layer, out_dtype=F32, prologue="silu", name="ada_down")
    return _matmul(t, b, layer, out_dtype=F32, epilogue="bias", vec=bias[:, None, :],
                   bn=_largest_divisor(b.shape[2], (2048, 1024, 512, 256, 128)), name="ada_up")


def kernel(x, c, ctx, c_ctx, ada_a, ada_b, ada_bias, norm_mix, norm_mlp, mlp_w1, mlp_w2,
           na_w_qkv, na_w_o, na_q_norm, na_k_norm, na_rel_bias,
           wa_w_qkv, wa_w_o, wa_q_norm, wa_k_norm, wa_sink,
           sg_w_in, sg_v_norm, sg_w_s, sg_b_s, sg_w_out):
    assert x.shape[0] == 1 and ctx.shape[0] == 1 and c.shape[0] == 1
    n_x, D = x.shape[1], x.shape[2]
    n_ctx = ctx.shape[1]
    depth = ada_a.shape[0]
    n_heads = D // HEAD_DIM
    n_mixers = 3

    xc = jnp.concatenate([x[0], ctx[0]], axis=0)
    cond = jnp.zeros((16, D), F32).at[0].set(c[0]).at[1].set(c_ctx)

    for i in range(depth):
        last = i == depth - 1
        kind, j = i % n_mixers, i // n_mixers
        mods = _ada_modulation(cond, ada_a, ada_b, ada_bias, i)
        gate = dict(epilogue="gate_res", vec=mods, n_x=n_x, out_dtype=F32,
                    m_rows=n_x if last else None)

        h = _norm_modulate(xc, norm_mix[i], mods, shift_col=0, scale_col=1, n_x=n_x)
        if kind == 0:
            qkv = _matmul(h, na_w_qkv, j, out_dtype=BF16, epilogue="head_norm",
                          vec=_head_gains(na_q_norm[j], na_k_norm[j]), norm_cols=(D, 2 * D),
                          name="na_qkv")
            o = _na_attention(qkv, na_rel_bias[j], n_x=n_x, n_ctx=n_ctx)
            xc = _matmul(o, na_w_o, j, res=xc, vec_col=2, name="na_out", **gate)
        elif kind == 1:
            qkv = _matmul(h, wa_w_qkv, j, out_dtype=BF16, epilogue="head_norm",
                          vec=_head_gains(wa_q_norm[j], wa_k_norm[j]),
                          norm_cols=(D, D + WA_KV_HEADS * HEAD_DIM), name="wa_qkv")
            o = _wa_attention(qkv, wa_sink[j], n_x=n_x, n_ctx=n_ctx, n_heads=n_heads)
            xc = _matmul(o, wa_w_o, j, res=xc, vec_col=2, name="wa_out", **gate)
        else:
            uv = _matmul(h, sg_w_in, j, out_dtype=BF16, epilogue="gelu", name="sg_in")
            z = _spatial_gate(uv, sg_v_norm[j], sg_w_s[j], sg_b_s[j])
            xc = _matmul(z, sg_w_out, j, res=xc, vec_col=2, name="sg_out", **gate)

        h2 = _norm_modulate(xc, norm_mlp[i], mods, shift_col=3, scale_col=4, n_x=n_x)
        hid = _matmul(h2, mlp_w1, i, out_dtype=BF16, epilogue="relu2", name="mlp_up")
        xc = _matmul(hid, mlp_w2, i, res=xc, vec_col=5, name="mlp_down", **gate)
    return xc[None]
```

```python
import functools

import numpy as np
import jax
import jax.numpy as jnp
from jax import lax
from jax.experimental import pallas as pl
from jax.experimental.pallas import tpu as pltpu

GRID_W = 64
HEAD_DIM = 128
NA_KH = 8
NA_KW = 16
WA_KV_HEADS = 8
WA_WINDOW = 128
ROPE_BASE = 10000.0
SG_GROUPS = 8
SG_CHUNK = 128
N_MOD = 6
EPS = 1e-6
NEG_INF = -1e30
LOG2E = np.float32(np.log2(np.e))
PAIRS_PER_BODY = 8
ROW_CHUNKS = 4
SCORE_SLOTS = 4

LANES = 128
V7X_VMEM_BYTES = 64 * 1024 * 1024
VMEM_CAP = V7X_VMEM_BYTES - 6 * 1024 * 1024
VMEM_SLACK = 2 * 1024 * 1024

F32 = jnp.float32
BF16 = jnp.bfloat16
_NT = (((1,), (1,)), ((), ()))


def _vmem_limit(nbytes):
    return int(min(VMEM_CAP, nbytes + 4 * VMEM_SLACK))


def _largest_divisor(n, candidates):
    for c in candidates:
        if n % c == 0:
            return c
    raise ValueError(f"no tile in {candidates} divides {n}")


def _mm_body(*refs, nk, bm, epilogue, prologue, n_x, norm_tiles, row_chunks):
    x_ref, w_ref = refs[0], refs[1]
    pos = 2
    res_ref = vec_ref = None
    if epilogue == "gate_res":
        res_ref, vec_ref = refs[2], refs[3]
        pos = 4
    elif epilogue in ("bias", "head_norm"):
        vec_ref = refs[2]
        pos = 3
    o_ref = refs[pos]
    acc_ref = refs[pos + 1] if nk > 1 else None
    k = pl.program_id(1)
    m = pl.program_id(2)

    if nk > 1:
        rows = pl.ds(pl.multiple_of(m * bm, bm), bm)

        @pl.when(k == 0)
        def _zero():
            acc_ref[rows, :] = jnp.zeros((bm, acc_ref.shape[1]), F32)

    def matmul_rows(r0, nr):
        xv = x_ref[r0:r0 + nr, :]
        if prologue == "silu":
            xv = xv * jax.nn.sigmoid(xv)
        return jnp.dot(xv.astype(BF16), w_ref[...].astype(BF16), preferred_element_type=F32)

    def finish(acc, r0, nr):
        out_rows = slice(r0, r0 + nr)
        if epilogue == "relu2":
            y = jnp.square(jnp.maximum(acc, 0.0))
        elif epilogue == "gelu":
            y = 0.5 * acc * (1.0 + lax.erf(acc * np.float32(np.sqrt(0.5))))
        elif epilogue == "gate_res":
            row_id = m * bm + r0 + lax.broadcasted_iota(jnp.int32, (nr, 1), 0)
            gate = jnp.where(row_id >= n_x, vec_ref[1:2, :], vec_ref[0:1, :])
            y = res_ref[out_rows, :] + gate * acc
        elif epilogue == "bias":
            y = acc + vec_ref[...]
        elif epilogue == "head_norm":
            n = pl.program_id(0)
            gain = vec_ref[pl.ds((n >= norm_tiles[0]).astype(jnp.int32), 1), :]
            for t in range(acc.shape[1] // HEAD_DIM):
                cols = slice(t * HEAD_DIM, (t + 1) * HEAD_DIM)
                a = acc[:, cols]
                o_ref[out_rows, cols] = jnp.where(n < norm_tiles[1], _rms_gain(a, gain), a
                                                  ).astype(o_ref.dtype)
            return
        else:
            y = acc
        o_ref[out_rows, :] = y.astype(o_ref.dtype)

    nr = bm // row_chunks
    if nk == 1:
        for c in range(row_chunks):
            finish(matmul_rows(c * nr, nr), c * nr, nr)
    else:
        for c in range(row_chunks):
            chunk = pl.ds(pl.multiple_of(m * bm + c * nr, nr), nr)
            acc_ref[chunk, :] = acc_ref[chunk, :] + matmul_rows(c * nr, nr)

        @pl.when(k == nk - 1)
        def _emit():
            finish(acc_ref[rows, :], 0, bm)


def _matmul(x, w, layer, *, out_dtype, epilogue=None, prologue=None, res=None, vec=None, vec_col=0,
            n_x=0, m_rows=None, norm_cols=None, bm=None, bn=None, bk=None, name="mm"):
    M = x.shape[0] if m_rows is None else m_rows
    _, K, N = w.shape
    bn = bn or _largest_divisor(int(np.gcd.reduce((N,) + tuple(norm_cols or ()))), (512, 256, 128))
    bk = bk or _largest_divisor(K, (4096, 2048, 1024, 512, 256, 128))
    nk = K // bk
    last_k = nk - 1
    out_bytes = jnp.dtype(out_dtype).itemsize

    def vmem_bytes(bm_):
        n = 2 * bm_ * bk * x.dtype.itemsize + 2 * bk * bn * 4 + 2 * bm_ * bn * out_bytes
        if epilogue == "gate_res":
            n += 2 * bm_ * bn * 4 + 2 * vec.shape[0] * bn * 4
        if nk > 1:
            n += M * bn * 4
        return n

    if bm is None:
        bm = next(c for c in (1408, 1024, 768, 512, 384, 256, 128, 16)
                  if M % c == 0 and vmem_bytes(c) <= VMEM_CAP - VMEM_SLACK)

    def out_rows(k, m):
        return m if nk == 1 else jnp.where(k == last_k, m, 0)

    in_specs = [
        pl.BlockSpec((bm, bk), lambda n, k, m: (m, k)),
        pl.BlockSpec((None, bk, bn), lambda n, k, m: (layer, k, n)),
    ]
    args = [x, w]
    if epilogue == "gate_res":
        in_specs.append(pl.BlockSpec((bm, bn), lambda n, k, m: (out_rows(k, m), n)))
        in_specs.append(pl.BlockSpec((vec.shape[0], bn), lambda n, k, m: (0, vec_col * (N // bn) + n)))
        args += [res, vec]
    elif epilogue == "bias":
        in_specs.append(pl.BlockSpec((None, 1, bn), lambda n, k, m: (layer, 0, n)))
        args.append(vec)
    norm_tiles = None
    if epilogue == "head_norm":
        in_specs.append(pl.BlockSpec(vec.shape, lambda n, k, m: (0, 0)))
        args.append(vec)
        assert norm_cols[0] % bn == 0 and norm_cols[1] % bn == 0 and bn % HEAD_DIM == 0
        norm_tiles = (norm_cols[0] // bn, norm_cols[1] // bn)
    scratch = [pltpu.VMEM((M, bn), F32)] if nk > 1 else []
    row_chunks = 1 if nk > 1 else next(c for c in (ROW_CHUNKS, 2, 1) if bm % (16 * c) == 0)
    body = functools.partial(_mm_body, nk=nk, bm=bm, epilogue=epilogue, prologue=prologue, n_x=n_x,
                             norm_tiles=norm_tiles, row_chunks=row_chunks)
    return pl.pallas_call(
        body,
        grid=(N // bn, nk, M // bm),
        in_specs=in_specs,
        out_specs=pl.BlockSpec((bm, bn), lambda n, k, m: (out_rows(k, m), n)),
        out_shape=jax.ShapeDtypeStruct((M, N), out_dtype),
        scratch_shapes=scratch,
        compiler_params=pltpu.CompilerParams(
            dimension_semantics=("arbitrary", "arbitrary", "arbitrary"),
            vmem_limit_bytes=_vmem_limit(vmem_bytes(bm))),
        name=name,
    )(*args)


def _norm_body(x_ref, g_ref, sh_ref, sc_ref, o_ref, *, n_x):
    bm = x_ref.shape[0]
    chunk = 8
    assert n_x % chunk == 0

    def rows_chunk(c, carry):
        r = pl.ds(pl.multiple_of(c * chunk, chunk), chunk)
        t = (pl.program_id(0) * bm + c * chunk >= n_x).astype(jnp.int32)
        x = x_ref[r, :]
        y = x * lax.rsqrt(jnp.mean(x * x, axis=-1, keepdims=True) + EPS)
        y = y * g_ref[...]
        o_ref[r, :] = (y * (1.0 + sc_ref[pl.ds(t, 1), :]) + sh_ref[pl.ds(t, 1), :]).astype(o_ref.dtype)
        return carry

    lax.fori_loop(0, bm // chunk, rows_chunk, 0, unroll=4)


def _norm_modulate(x, gain, mods, *, shift_col, scale_col, n_x):
    M, D = x.shape
    bm = _largest_divisor(M, (1024, 768, 512, 384, 256, 128, 64, 32, 16))
    nbytes = 2 * bm * D * 4 + 2 * bm * D * 2 + 6 * 8 * D * 4
    return pl.pallas_call(
        functools.partial(_norm_body, n_x=n_x),
        grid=(M // bm,),
        in_specs=[
            pl.BlockSpec((bm, D), lambda i: (i, 0)),
            pl.BlockSpec((1, D), lambda i: (0, 0)),
            pl.BlockSpec((mods.shape[0], D), lambda i: (0, shift_col)),
            pl.BlockSpec((mods.shape[0], D), lambda i: (0, scale_col)),
        ],
        out_specs=pl.BlockSpec((bm, D), lambda i: (i, 0)),
        out_shape=jax.ShapeDtypeStruct((M, D), BF16),
        compiler_params=pltpu.CompilerParams(
            dimension_semantics=("arbitrary",), vmem_limit_bytes=_vmem_limit(nbytes)),
        name="norm_modulate",
    )(x, gain.reshape(1, D), mods, mods)


def _rms_gain(xf, gain):
    return (xf * lax.rsqrt(jnp.mean(xf * xf, axis=-1, keepdims=True) + EPS)) * gain


def _softmax_pv(parts, extra_logit=None):
    m_tile = None
    for s, _ in parts:
        for t in range(s.shape[1] // LANES):
            st = s[:, t * LANES:(t + 1) * LANES]
            m_tile = st if m_tile is None else jnp.maximum(m_tile, st)
    m = jnp.max(m_tile, axis=-1, keepdims=True)
    if extra_logit is not None:
        m = jnp.maximum(m, extra_logit)
    acc = None
    for s, v in parts:
        v_ones = jnp.concatenate([v, jnp.ones_like(v)], axis=1)
        ov = jnp.dot(jnp.exp2(s - m).astype(BF16), v_ones, preferred_element_type=F32)
        acc = ov if acc is None else acc + ov
    d = acc.shape[1] // 2
    o, l = acc[:, :d], acc[:, d:]
    if extra_logit is not None:
        l = l + jnp.exp2(extra_logit - m)
    return o / l


def _na_bias_planes(rb_ref, tb_ref, kh):
    W, kw = GRID_W, NA_KW
    shape = (W, 2 * W)
    qc = lax.broadcasted_iota(jnp.int32, shape, 0)
    lane = lax.broadcasted_iota(jnp.int32, shape, 1)
    kc = lane % W
    win = jnp.clip(qc - kw // 2, 0, W - kw)
    in_window = (kc >= win) & (kc < win + kw)
    left = lane < W
    neg = jnp.full(shape, NEG_INF, F32)

    def toeplitz(d, lane0):
        if d < 0 or d > 2 * kh - 2:
            return neg
        row = jnp.broadcast_to(rb_ref[0, d:d + 1, :], shape)
        return pltpu.roll(row, (lane0 - (kw - 1)) % LANES, 1, stride=1, stride_axis=0)

    for p in range(2 * kh + 1):
        both = jnp.where(in_window, jnp.where(left, toeplitz(p - 1, 0), toeplitz(p, W)), NEG_INF)
        tb_ref[0, p] = both
        tb_ref[1, p] = jnp.where(left, both, NEG_INF)
        tb_ref[2, p] = jnp.where(left, NEG_INF, both)


def _na_body(q_ref, k_ref, v_ref, rb_ref, o_ref, tb_ref, sl_ref, sc_ref, *, n_x, n_ctx, rows, kh):
    W = GRID_W
    nkr = kh + 2
    n_steps = rows // 2
    kc = k_ref[n_x:n_x + n_ctx, :]
    vc = v_ref[n_x:n_x + n_ctx, :]
    _na_bias_planes(rb_ref, tb_ref, kh)

    def windows(i):
        r = 2 * i
        a = jnp.clip(r - kh // 2, 0, rows - nkr)
        return (r, a, pl.ds(pl.multiple_of(r * W, 2 * W), 2 * W),
                pl.ds(pl.multiple_of(a * W, 2 * W), nkr * W))

    def scores(i, slot):
        r, a, qrows, krows = windows(i)
        q2 = q_ref[qrows, :]
        s_loc = lax.dot_general(q2, k_ref[krows, :], _NT, preferred_element_type=F32)
        bias_rows = []
        for j in range(2):
            qr = r + j
            r0 = jnp.clip(qr - kh // 2, 0, rows - kh)
            pieces = []
            for t in range(nkr // 2):
                kr0 = a + 2 * t
                in0 = (kr0 >= r0) & (kr0 < r0 + kh)
                in1 = (kr0 + 1 >= r0) & (kr0 + 1 < r0 + kh)
                variant = jnp.where(in0 & in1, 0, jnp.where(in0, 1, 2))
                plane = jnp.where(in0 | in1, jnp.clip(kr0 - qr + kh, 0, 2 * kh - 1), 2 * kh)
                pieces.append(tb_ref[variant, plane])
            bias_rows.append(jnp.concatenate(pieces, axis=1))
        sl_ref[slot] = s_loc + jnp.concatenate(bias_rows, axis=0)
        sc_ref[slot] = lax.dot_general(q2, kc, _NT, preferred_element_type=F32)

    def attend(i, slot):
        _, _, qrows, krows = windows(i)
        o = _softmax_pv([(sl_ref[slot], v_ref[krows, :]), (sc_ref[slot], vc)])
        o_ref[qrows, :] = o.astype(o_ref.dtype)

    scores(0, 0)
    scores(1, 1)

    def pairs(u, carry):
        for ph in range(PAIRS_PER_BODY):
            t = 2 * (PAIRS_PER_BODY * u + ph)
            cur = 2 * (ph % 2)
            for j in range(2):
                scores(jnp.minimum(t + 2 + j, n_steps - 1), 2 - cur + j)
                attend(t + j, cur + j)
        return carry

    lax.fori_loop(0, n_steps // (2 * PAIRS_PER_BODY), pairs, 0)

    ctx_rows = slice(n_x, n_x + n_ctx)
    s = lax.dot_general(q_ref[ctx_rows, :], kc, _NT, preferred_element_type=F32)
    o_ref[ctx_rows, :] = _softmax_pv([(s, vc)]).astype(o_ref.dtype)

def _na_attention(qkv, rel_bias, *, n_x, n_ctx):
    mtot, width = qkv.shape
    H = width // (3 * HEAD_DIM)
    rows = n_x // GRID_W
    kh = min(NA_KH, rows)
    assert GRID_W * 2 == LANES and rows % (4 * PAIRS_PER_BODY) == 0 and rows >= kh + 2
    assert kh == NA_KH and kh % 2 == 0 and PAIRS_PER_BODY % 2 == 0 and SCORE_SLOTS == 4
    assert rel_bias.shape == (H, 2 * kh - 1, 2 * NA_KW - 1) and 2 * NA_KW - 1 <= LANES
    rb = jnp.pad(rel_bias.astype(F32) * LOG2E, ((0, 0), (0, 1), (0, LANES - (2 * NA_KW - 1))))
    blk = (mtot, HEAD_DIM)
    nq, nk = 2 * GRID_W, (kh + 2) * GRID_W
    plane_shape = (3, 2 * kh + 1, GRID_W, 2 * GRID_W)
    nbytes = (8 * mtot * HEAD_DIM * 2 + int(np.prod(plane_shape)) * 4
              + SCORE_SLOTS * nq * (nk + n_ctx) * 4)
    return pl.pallas_call(
        functools.partial(_na_body, n_x=n_x, n_ctx=n_ctx, rows=rows, kh=kh),
        grid=(H,),
        in_specs=[
            pl.BlockSpec(blk, lambda h: (0, h)),
            pl.BlockSpec(blk, lambda h: (0, H + h)),
            pl.BlockSpec(blk, lambda h: (0, 2 * H + h)),
            pl.BlockSpec((1,) + rb.shape[1:], lambda h: (h, 0, 0)),
        ],
        out_specs=pl.BlockSpec(blk, lambda h: (0, h)),
        out_shape=jax.ShapeDtypeStruct((mtot, H * HEAD_DIM), BF16),
        scratch_shapes=[pltpu.VMEM(plane_shape, F32), pltpu.VMEM((SCORE_SLOTS, nq, nk), F32),
                        pltpu.VMEM((SCORE_SLOTS, nq, n_ctx), F32)],
        compiler_params=pltpu.CompilerParams(
            dimension_semantics=("arbitrary",), vmem_limit_bytes=_vmem_limit(nbytes)),
        name="na_attention",
    )(qkv, qkv, qkv, rb)


def _wa_body(q_ref, k_ref, v_ref, ca_ref, sa_ref, cb_ref, sb_ref, band_ref, sink_ref,
             o_ref, kn_ref, sl_ref, sc_ref, *, n_x, n_ctx, group):
    W = WA_WINDOW
    rpb = W // GRID_W
    nblk = n_x // W
    g = pl.program_id(1)
    lane = lax.broadcasted_iota(jnp.int32, (W, HEAD_DIM), 1)
    neg_half = (lane % (HEAD_DIM // 2)) < HEAD_DIM // 4

    def rope(y, b):
        def table(row_ref, col_ref):
            row_part = [jnp.broadcast_to(row_ref[pl.ds(rpb * b + j, 1), :], (GRID_W, HEAD_DIM))
                        for j in range(rpb)]
            return jnp.concatenate(row_part, axis=0) + jnp.concatenate([col_ref[...]] * rpb, axis=0)
        rot = jnp.where(neg_half, -pltpu.roll(y, HEAD_DIM - HEAD_DIM // 4, 1),
                        pltpu.roll(y, HEAD_DIM // 4, 1))
        return y * table(ca_ref, cb_ref) + rot * table(sa_ref, sb_ref)

    @pl.when(g == 0)
    def _prepare_keys():
        def kblock(b, carry):
            r = pl.ds(pl.multiple_of(b * W, W), W)
            kn_ref[r, :] = rope(k_ref[r, :].astype(F32), b).astype(BF16)
            return carry
        lax.fori_loop(0, nblk, kblock, 0, unroll=2)
        kn_ref[n_x:n_x + n_ctx, :] = k_ref[n_x:n_x + n_ctx, :]

    sink = sink_ref[pl.program_id(0) * group + g] * LOG2E
    kc = kn_ref[n_x:n_x + n_ctx, :]
    vc = v_ref[n_x:n_x + n_ctx, :]

    def windows(b):
        a = jnp.clip((b - 1) * W, 0, n_x - 3 * W)
        return pl.ds(pl.multiple_of(b * W, W), W), pl.ds(pl.multiple_of(a, W), 3 * W)

    def scores(b, slot):
        r, krows = windows(b)
        qn = rope(q_ref[r, :].astype(F32), b).astype(BF16)
        s_loc = lax.dot_general(qn, kn_ref[krows, :], _NT, preferred_element_type=F32)
        edge = jnp.where(b == 0, 0, jnp.where(b == nblk - 1, 2, 1))
        sl_ref[slot] = s_loc + band_ref[edge]
        sc_ref[slot] = lax.dot_general(qn, kc, _NT, preferred_element_type=F32)

    def attend(b, slot):
        r, krows = windows(b)
        o = _softmax_pv([(sl_ref[slot], v_ref[krows, :]), (sc_ref[slot], vc)], extra_logit=sink)
        o_ref[r, :] = o.astype(o_ref.dtype)

    scores(0, 0)
    scores(1, 1)

    def pairs(u, carry):
        for ph in range(PAIRS_PER_BODY):
            t = 2 * (PAIRS_PER_BODY * u + ph)
            cur = 2 * (ph % 2)
            for j in range(2):
                scores(jnp.minimum(t + 2 + j, nblk - 1), 2 - cur + j)
                attend(t + j, cur + j)
        return carry

    lax.fori_loop(0, nblk // (2 * PAIRS_PER_BODY), pairs, 0)

    for c in range(n_ctx // W):
        r = slice(n_x + c * W, n_x + (c + 1) * W)
        s = lax.dot_general(q_ref[r, :], kc, _NT, preferred_element_type=F32)
        o_ref[r, :] = _softmax_pv([(s, vc)], extra_logit=sink).astype(o_ref.dtype)


def _rope_tables(rows):
    axis_dim = HEAD_DIM // 2
    inv = ROPE_BASE ** (-jnp.arange(0, axis_dim, 2, dtype=F32) / axis_dim)
    ar = jnp.arange(rows, dtype=F32)[:, None] * inv
    ac = jnp.arange(GRID_W, dtype=F32)[:, None] * inv
    zr = jnp.zeros((rows, axis_dim), F32)
    zc = jnp.zeros((GRID_W, axis_dim), F32)

    def row_tab(f):
        return jnp.concatenate([f(ar), f(ar), zr], axis=-1)

    def col_tab(f):
        return jnp.concatenate([zc, f(ac), f(ac)], axis=-1)

    return row_tab(jnp.cos), row_tab(jnp.sin), col_tab(jnp.cos), col_tab(jnp.sin)


def _wa_band_bias():
    W = WA_WINDOW
    dist = np.arange(3 * W)[None, :] - np.arange(W)[:, None]
    return np.stack([np.where(np.abs(dist + d) <= W, 0.0, NEG_INF) for d in (0, -W, -2 * W)]
                    ).astype(np.float32)


def _wa_attention(qkv, sink, *, n_x, n_ctx, n_heads):
    mtot = qkv.shape[0]
    H, KVH = n_heads, WA_KV_HEADS
    G = H // KVH
    W = WA_WINDOW
    assert W % GRID_W == 0 and n_x % (2 * PAIRS_PER_BODY * W) == 0 and n_ctx % W == 0
    assert W == HEAD_DIM and PAIRS_PER_BODY % 2 == 0 and SCORE_SLOTS == 4
    ca, sa, cb, sb = _rope_tables(n_x // GRID_W)
    band = jnp.asarray(_wa_band_bias())
    blk = (mtot, HEAD_DIM)
    full = lambda arr: pl.BlockSpec(arr.shape, lambda kv, g: (0,) * arr.ndim)
    nbytes = (8 * mtot * HEAD_DIM * 2 + mtot * HEAD_DIM * 2 + 4 * (ca.size + cb.size) * 4
              + 2 * band.size * 4 + SCORE_SLOTS * W * (3 * W + n_ctx) * 4)
    return pl.pallas_call(
        functools.partial(_wa_body, n_x=n_x, n_ctx=n_ctx, group=G),
        grid=(KVH, G),
        in_specs=[
            pl.BlockSpec(blk, lambda kv, g: (0, kv * G + g)),
            pl.BlockSpec(blk, lambda kv, g: (0, H + kv)),
            pl.BlockSpec(blk, lambda kv, g: (0, H + KVH + kv)),
            full(ca), full(sa), full(cb), full(sb), full(band),
            pl.BlockSpec(memory_space=pltpu.SMEM),
        ],
        out_specs=pl.BlockSpec(blk, lambda kv, g: (0, kv * G + g)),
        out_shape=jax.ShapeDtypeStruct((mtot, H * HEAD_DIM), BF16),
        scratch_shapes=[pltpu.VMEM(blk, BF16), pltpu.VMEM((SCORE_SLOTS, W, 3 * W), F32),
                        pltpu.VMEM((SCORE_SLOTS, W, n_ctx), F32)],
        compiler_params=pltpu.CompilerParams(
            dimension_semantics=("arbitrary", "arbitrary"), vmem_limit_bytes=_vmem_limit(nbytes)),
        name="wa_attention",
    )(qkv, qkv, qkv, ca, sa, cb, sb, band, sink.astype(F32))


def _sg_body(u_ref, v_ref, vg_ref, ws_ref, bs_ref, o_ref, *, n_chunks, groups):
    P = SG_CHUNK
    gw = v_ref.shape[1] // groups
    vn = _rms_gain(v_ref[...].astype(F32), vg_ref[...]).astype(BF16)
    for g in range(groups):
        wg = ws_ref[g].astype(BF16)
        cols = slice(g * gw, (g + 1) * gw)
        for c in range(n_chunks):
            rows = slice(c * P, (c + 1) * P)
            sv = jnp.dot(wg, vn[rows, cols], preferred_element_type=F32) + bs_ref[g]
            o_ref[rows, cols] = (u_ref[rows, cols].astype(F32) * sv).astype(o_ref.dtype)


def _spatial_gate(uv, v_gain, w_s, b_s):
    mtot, two_d = uv.shape
    D = two_d // 2
    G, P = SG_GROUPS, SG_CHUNK
    gw = D // G
    assert gw % LANES == 0 and mtot % P == 0
    n_chunks = _largest_divisor(mtot // P, (3, 2, 1))
    bm = n_chunks * P
    bias = jnp.broadcast_to(b_s.astype(F32)[:, :, None], (G, P, gw))
    nbytes = 6 * bm * D * 2 + bm * D * 6 + 2 * (w_s.size + bias.size) * 4
    return pl.pallas_call(
        functools.partial(_sg_body, n_chunks=n_chunks, groups=G),
        grid=(mtot // bm,),
        in_specs=[
            pl.BlockSpec((bm, D), lambda i: (i, 0)),
            pl.BlockSpec((bm, D), lambda i: (i, 1)),
            pl.BlockSpec((1, D), lambda i: (0, 0)),
            pl.BlockSpec(w_s.shape, lambda i: (0, 0, 0)),
            pl.BlockSpec(bias.shape, lambda i: (0, 0, 0)),
        ],
        out_specs=pl.BlockSpec((bm, D), lambda i: (i, 0)),
        out_shape=jax.ShapeDtypeStruct((mtot, D), BF16),
        compiler_params=pltpu.CompilerParams(
            dimension_semantics=("arbitrary",), vmem_limit_bytes=_vmem_limit(nbytes)),
        name="spatial_gate",
    )(uv, uv, v_gain.reshape(1, D), w_s, bias)


def _head_gains(q_gain, k_gain):
    scale = np.float32(HEAD_DIM ** -0.5) * LOG2E
    rows = jnp.stack([q_gain.astype(F32) * scale, k_gain.astype(F32)])
    return jnp.concatenate([rows, jnp.zeros((6, HEAD_DIM), F32)], axis=0)


def _ada_modulation(cond, a, b, bias, layer):
    t = _matmul(cond, a, layer, out_dtype=F32, prologue="silu", name="ada_down")
    return _matmul(t, b, layer, out_dtype=F32, epilogue="bias", vec=bias[:, None, :],
                   bn=_largest_divisor(b.shape[2], (2048, 1024, 512, 256, 128)), name="ada_up")


def kernel(x, c, ctx, c_ctx, ada_a, ada_b, ada_bias, norm_mix, norm_mlp, mlp_w1, mlp_w2,
           na_w_qkv, na_w_o, na_q_norm, na_k_norm, na_rel_bias,
           wa_w_qkv, wa_w_o, wa_q_norm, wa_k_norm, wa_sink,
           sg_w_in, sg_v_norm, sg_w_s, sg_b_s, sg_w_out):
    assert x.shape[0] == 1 and ctx.shape[0] == 1 and c.shape[0] == 1
    n_x, D = x.shape[1], x.shape[2]
    n_ctx = ctx.shape[1]
    depth = ada_a.shape[0]
    n_heads = D // HEAD_DIM
    n_mixers = 3

    xc = jnp.concatenate([x[0], ctx[0]], axis=0)
    cond = jnp.zeros((16, D), F32).at[0].set(c[0]).at[1].set(c_ctx)

    for i in range(depth):
        last = i == depth - 1
        kind, j = i % n_mixers, i // n_mixers
        mods = _ada_modulation(cond, ada_a, ada_b, ada_bias, i)
        gate = dict(epilogue="gate_res", vec=mods, n_x=n_x, out_dtype=F32,
                    m_rows=n_x if last else None)

        h = _norm_modulate(xc, norm_mix[i], mods, shift_col=0, scale_col=1, n_x=n_x)
        if kind == 0:
            qkv = _matmul(h, na_w_qkv, j, out_dtype=BF16, epilogue="head_norm",
                          vec=_head_gains(na_q_norm[j], na_k_norm[j]), norm_cols=(D, 2 * D),
                          name="na_qkv")
            o = _na_attention(qkv, na_rel_bias[j], n_x=n_x, n_ctx=n_ctx)
            xc = _matmul(o, na_w_o, j, res=xc, vec_col=2, name="na_out", **gate)
        elif kind == 1:
            qkv = _matmul(h, wa_w_qkv, j, out_dtype=BF16, epilogue="head_norm",
                          vec=_head_gains(wa_q_norm[j], wa_k_norm[j]),
                          norm_cols=(D, D + WA_KV_HEADS * HEAD_DIM), name="wa_qkv")
            o = _wa_attention(qkv, wa_sink[j], n_x=n_x, n_ctx=n_ctx, n_heads=n_heads)
            xc = _matmul(o, wa_w_o, j, res=xc, vec_col=2, name="wa_out", **gate)
        else:
            uv = _matmul(h, sg_w_in, j, out_dtype=BF16, epilogue="gelu", name="sg_in")
            z = _spatial_gate(uv, sg_v_norm[j], sg_w_s[j], sg_b_s[j])
            xc = _matmul(z, sg_w_out, j, res=xc, vec_col=2, name="sg_out", **gate)

        h2 = _norm_modulate(xc, norm_mlp[i], mods, shift_col=3, scale_col=4, n_x=n_x)
        hid = _matmul(h2, mlp_w1, i, out_dtype=BF16, epilogue="relu2", name="mlp_up")
        xc = _matmul(hid, mlp_w2, i, res=xc, vec_col=5, name="mlp_down", **gate)
    return xc[None]
```

```python
import functools

import numpy as np
import jax
import jax.numpy as jnp
from jax import lax
from jax.experimental import pallas as pl
from jax.experimental.pallas import tpu as pltpu

GRID_W = 64
HEAD_DIM = 128
NA_KH = 8
NA_KW = 16
WA_KV_HEADS = 8
WA_WINDOW = 128
ROPE_BASE = 10000.0
SG_GROUPS = 8
SG_CHUNK = 128
N_MOD = 6
EPS = 1e-6
NEG_INF = -1e30
LOG2E = np.float32(np.log2(np.e))
PAIRS_PER_BODY = 8
MIN_CHUNK_ROWS = 128
ROW_CHUNKS = 4
SCORE_SLOTS = 4

LANES = 128
V7X_VMEM_BYTES = 64 * 1024 * 1024
VMEM_CAP = V7X_VMEM_BYTES - 2 * 1024 * 1024
VMEM_SLACK = 2 * 1024 * 1024

F32 = jnp.float32
BF16 = jnp.bfloat16
_NT = (((1,), (1,)), ((), ()))


def _vmem_limit(nbytes):
    return int(min(VMEM_CAP, nbytes + 4 * VMEM_SLACK))


def _row_chunks(bm):
    if bm % (16 * ROW_CHUNKS) or bm // ROW_CHUNKS < MIN_CHUNK_ROWS:
        return (bm,)
    return (bm // ROW_CHUNKS,) * ROW_CHUNKS


def _largest_divisor(n, candidates):
    for c in candidates:
        if n % c == 0:
            return c
    raise ValueError(f"no tile in {candidates} divides {n}")


def _mm_body(*refs, nk, bm, epilogue, prologue, n_x, norm_tiles, row_chunks):
    x_ref, w_ref = refs[0], refs[1]
    pos = 2
    res_ref = vec_ref = None
    if epilogue == "gate_res":
        res_ref, vec_ref = refs[2], refs[3]
        pos = 4
    elif epilogue in ("bias", "head_norm"):
        vec_ref = refs[2]
        pos = 3
    o_ref = refs[pos]
    acc_ref = refs[pos + 1] if nk > 1 else None
    k = pl.program_id(1)
    m = pl.program_id(2)

    if nk > 1:
        rows = pl.ds(pl.multiple_of(m * bm, bm), bm)

        @pl.when(k == 0)
        def _zero():
            acc_ref[rows, :] = jnp.zeros((bm, acc_ref.shape[1]), F32)

    def matmul_rows(r0, nr):
        xv = x_ref[r0:r0 + nr, :]
        if prologue == "silu":
            xv = xv * jax.nn.sigmoid(xv)
        return jnp.dot(xv.astype(BF16), w_ref[...].astype(BF16), preferred_element_type=F32)

    def finish(acc, r0, nr):
        out_rows = slice(r0, r0 + nr)
        if epilogue == "relu2":
            y = jnp.square(jnp.maximum(acc, 0.0))
        elif epilogue == "gelu":
            y = 0.5 * acc * (1.0 + lax.erf(acc * np.float32(np.sqrt(0.5))))
        elif epilogue == "gate_res":
            row_id = m * bm + r0 + lax.broadcasted_iota(jnp.int32, (nr, 1), 0)
            gate = jnp.where(row_id >= n_x, vec_ref[1:2, :], vec_ref[0:1, :])
            y = res_ref[out_rows, :] + gate * acc
        elif epilogue == "bias":
            y = acc + vec_ref[...]
        elif epilogue == "head_norm":
            n = pl.program_id(0)
            kind = (n >= norm_tiles[0]).astype(jnp.int32) + (n >= norm_tiles[1]).astype(jnp.int32)
            gain = vec_ref[pl.ds(kind, 1), :]
            for t in range(acc.shape[1] // HEAD_DIM):
                cols = slice(t * HEAD_DIM, (t + 1) * HEAD_DIM)
                a = acc[:, cols]
                inv_rms = lax.rsqrt(jnp.mean(a * a, axis=-1, keepdims=True) + EPS)
                inv_rms = jnp.where(kind < 2, inv_rms, 1.0)
                o_ref[out_rows, cols] = ((a * inv_rms) * gain).astype(o_ref.dtype)
            return
        else:
            y = acc
        o_ref[out_rows, :] = y.astype(o_ref.dtype)

    if nk == 1:
        r0 = 0
        for nr in row_chunks:
            finish(matmul_rows(r0, nr), r0, nr)
            r0 += nr
    else:
        acc_ref[rows, :] = acc_ref[rows, :] + matmul_rows(0, bm)

        @pl.when(k == nk - 1)
        def _emit():
            finish(acc_ref[rows, :], 0, bm)


def _matmul(x, w, layer, *, out_dtype, epilogue=None, prologue=None, res=None, vec=None, vec_col=0,
            n_x=0, m_rows=None, norm_cols=None, bm=None, bn=None, bk=None, name="mm"):
    M = x.shape[0] if m_rows is None else m_rows
    _, K, N = w.shape
    bn = bn or _largest_divisor(int(np.gcd.reduce((N,) + tuple(norm_cols or ()))), (512, 256, 128))
    bk = bk or _largest_divisor(K, (4096, 2048, 1024, 512, 256, 128))
    nk = K // bk
    last_k = nk - 1
    out_bytes = jnp.dtype(out_dtype).itemsize

    def vmem_bytes(bm_):
        n = 2 * bm_ * bk * x.dtype.itemsize + 2 * bk * bn * 4 + 2 * bm_ * bn * out_bytes
        if epilogue == "gate_res":
            n += 2 * bm_ * bn * 4 + 2 * vec.shape[0] * bn * 4
        if nk > 1:
            n += M * bn * 4
        return n

    if bm is None:
        bm = next(c for c in (1408, 1056, 1024, 768, 512, 384, 256, 128, 16)
                  if M % c == 0 and vmem_bytes(c) <= VMEM_CAP - VMEM_SLACK)

    def out_rows(k, m):
        return m if nk == 1 else jnp.where(k == last_k, m, 0)

    in_specs = [
        pl.BlockSpec((bm, bk), lambda n, k, m: (m, k)),
        pl.BlockSpec((None, bk, bn), lambda n, k, m: (layer, k, n)),
    ]
    args = [x, w]
    if epilogue == "gate_res":
        in_specs.append(pl.BlockSpec((bm, bn), lambda n, k, m: (out_rows(k, m), n)))
        in_specs.append(pl.BlockSpec((vec.shape[0], bn), lambda n, k, m: (0, vec_col * (N // bn) + n)))
        args += [res, vec]
    elif epilogue == "bias":
        in_specs.append(pl.BlockSpec((None, 1, bn), lambda n, k, m: (layer, 0, n)))
        args.append(vec)
    norm_tiles = None
    if epilogue == "head_norm":
        in_specs.append(pl.BlockSpec(vec.shape, lambda n, k, m: (0, 0)))
        args.append(vec)
        assert norm_cols[0] % bn == 0 and norm_cols[1] % bn == 0 and bn % HEAD_DIM == 0
        norm_tiles = (norm_cols[0] // bn, norm_cols[1] // bn)
    scratch = [pltpu.VMEM((M, bn), F32)] if nk > 1 else []
    row_chunks = _row_chunks(bm)
    body = functools.partial(_mm_body, nk=nk, bm=bm, epilogue=epilogue, prologue=prologue, n_x=n_x,
                             norm_tiles=norm_tiles, row_chunks=row_chunks)
    return pl.pallas_call(
        body,
        grid=(N // bn, nk, M // bm),
        in_specs=in_specs,
        out_specs=pl.BlockSpec((bm, bn), lambda n, k, m: (out_rows(k, m), n)),
        out_shape=jax.ShapeDtypeStruct((M, N), out_dtype),
        scratch_shapes=scratch,
        compiler_params=pltpu.CompilerParams(
            dimension_semantics=("arbitrary", "arbitrary", "arbitrary"),
            vmem_limit_bytes=_vmem_limit(vmem_bytes(bm))),
        name=name,
    )(*args)


def _norm_body(x_ref, g_ref, sh_ref, sc_ref, o_ref, *, n_x):
    bm = x_ref.shape[0]
    chunk = 8
    assert n_x % chunk == 0

    def rows_chunk(c, carry):
        r = pl.ds(pl.multiple_of(c * chunk, chunk), chunk)
        t = (pl.program_id(0) * bm + c * chunk >= n_x).astype(jnp.int32)
        x = x_ref[r, :]
        y = x * lax.rsqrt(jnp.mean(x * x, axis=-1, keepdims=True) + EPS)
        y = y * g_ref[...]
        o_ref[r, :] = (y * (1.0 + sc_ref[pl.ds(t, 1), :]) + sh_ref[pl.ds(t, 1), :]).astype(o_ref.dtype)
        return carry

    lax.fori_loop(0, bm // chunk, rows_chunk, 0, unroll=4)


def _norm_modulate(x, gain, mods, *, shift_col, scale_col, n_x):
    M, D = x.shape
    bm = _largest_divisor(M, (1024, 768, 512, 384, 256, 128, 64, 32, 16))
    nbytes = 2 * bm * D * 4 + 2 * bm * D * 2 + 6 * 8 * D * 4
    return pl.pallas_call(
        functools.partial(_norm_body, n_x=n_x),
        grid=(M // bm,),
        in_specs=[
            pl.BlockSpec((bm, D), lambda i: (i, 0)),
            pl.BlockSpec((1, D), lambda i: (0, 0)),
            pl.BlockSpec((mods.shape[0], D), lambda i: (0, shift_col)),
            pl.BlockSpec((mods.shape[0], D), lambda i: (0, scale_col)),
        ],
        out_specs=pl.BlockSpec((bm, D), lambda i: (i, 0)),
        out_shape=jax.ShapeDtypeStruct((M, D), BF16),
        compiler_params=pltpu.CompilerParams(
            dimension_semantics=("arbitrary",), vmem_limit_bytes=_vmem_limit(nbytes)),
        name="norm_modulate",
    )(x, gain.reshape(1, D), mods, mods)


def _rms_gain(xf, gain):
    return (xf * lax.rsqrt(jnp.mean(xf * xf, axis=-1, keepdims=True) + EPS)) * gain


def _softmax_pv(parts, extra_logit=None):
    m_tile = None
    for s, _ in parts:
        for t in range(s.shape[1] // LANES):
            st = s[:, t * LANES:(t + 1) * LANES]
            m_tile = st if m_tile is None else jnp.maximum(m_tile, st)
    m = jnp.max(m_tile, axis=-1, keepdims=True)
    if extra_logit is not None:
        m = jnp.maximum(m, extra_logit)
    acc = None
    for s, v in parts:
        v_ones = jnp.concatenate([v, jnp.ones_like(v)], axis=1)
        ov = jnp.dot(jnp.exp2(s - m).astype(BF16), v_ones, preferred_element_type=F32)
        acc = ov if acc is None else acc + ov
    d = acc.shape[1] // 2
    o, l = acc[:, :d], acc[:, d:]
    if extra_logit is not None:
        l = l + jnp.exp2(extra_logit - m)
    return o / l


def _na_bias_planes(rb_ref, tb_ref, kh):
    W, kw = GRID_W, NA_KW
    shape = (W, 2 * W)
    qc = lax.broadcasted_iota(jnp.int32, shape, 0)
    lane = lax.broadcasted_iota(jnp.int32, shape, 1)
    kc = lane % W
    win = jnp.clip(qc - kw // 2, 0, W - kw)
    in_window = (kc >= win) & (kc < win + kw)
    left = lane < W
    neg = jnp.full(shape, NEG_INF, F32)

    def toeplitz(d, lane0):
        if d < 0 or d > 2 * kh - 2:
            return neg
        row = jnp.broadcast_to(rb_ref[0, d:d + 1, :], shape)
        return pltpu.roll(row, (lane0 - (kw - 1)) % LANES, 1, stride=1, stride_axis=0)

    for p in range(2 * kh + 1):
        both = jnp.where(in_window, jnp.where(left, toeplitz(p - 1, 0), toeplitz(p, W)), NEG_INF)
        tb_ref[0, p] = both
        tb_ref[1, p] = jnp.where(left, both, NEG_INF)
        tb_ref[2, p] = jnp.where(left, NEG_INF, both)


def _na_body(q_ref, k_ref, v_ref, rb_ref, o_ref, tb_ref, sl_ref, sc_ref, *, n_x, n_ctx, rows, kh):
    W = GRID_W
    nkr = kh + 2
    n_steps = rows // 2
    kc = k_ref[n_x:n_x + n_ctx, :]
    vc = v_ref[n_x:n_x + n_ctx, :]
    _na_bias_planes(rb_ref, tb_ref, kh)

    def windows(i):
        r = 2 * i
        a = jnp.clip(r - kh // 2, 0, rows - nkr)
        return (r, a, pl.ds(pl.multiple_of(r * W, 2 * W), 2 * W),
                pl.ds(pl.multiple_of(a * W, 2 * W), nkr * W))

    def scores(i, slot):
        r, a, qrows, krows = windows(i)
        q2 = q_ref[qrows, :]
        s_loc = lax.dot_general(q2, k_ref[krows, :], _NT, preferred_element_type=F32)
        bias_rows = []
        for j in range(2):
            qr = r + j
            r0 = jnp.clip(qr - kh // 2, 0, rows - kh)
            pieces = []
            for t in range(nkr // 2):
                kr0 = a + 2 * t
                in0 = (kr0 >= r0) & (kr0 < r0 + kh)
                in1 = (kr0 + 1 >= r0) & (kr0 + 1 < r0 + kh)
                variant = jnp.where(in0 & in1, 0, jnp.where(in0, 1, 2))
                plane = jnp.where(in0 | in1, jnp.clip(kr0 - qr + kh, 0, 2 * kh - 1), 2 * kh)
                pieces.append(tb_ref[variant, plane])
            bias_rows.append(jnp.concatenate(pieces, axis=1))
        sl_ref[slot] = s_loc + jnp.concatenate(bias_rows, axis=0)
        sc_ref[slot] = lax.dot_general(q2, kc, _NT, preferred_element_type=F32)

    def attend(i, slot):
        _, _, qrows, krows = windows(i)
        o = _softmax_pv([(sl_ref[slot], v_ref[krows, :]), (sc_ref[slot], vc)])
        o_ref[qrows, :] = o.astype(o_ref.dtype)

    scores(0, 0)
    scores(1, 1)

    def pairs(u, carry):
        for ph in range(PAIRS_PER_BODY):
            t = 2 * (PAIRS_PER_BODY * u + ph)
            cur = 2 * (ph % 2)
            for j in range(2):
                scores(jnp.minimum(t + 2 + j, n_steps - 1), 2 - cur + j)
                attend(t + j, cur + j)
        return carry

    lax.fori_loop(0, n_steps // (2 * PAIRS_PER_BODY), pairs, 0)

    ctx_rows = slice(n_x, n_x + n_ctx)
    s = lax.dot_general(q_ref[ctx_rows, :], kc, _NT, preferred_element_type=F32)
    o_ref[ctx_rows, :] = _softmax_pv([(s, vc)]).astype(o_ref.dtype)

def _na_attention(qkv, rel_bias, *, n_x, n_ctx):
    mtot, width = qkv.shape
    H = width // (3 * HEAD_DIM)
    rows = n_x // GRID_W
    kh = min(NA_KH, rows)
    assert GRID_W * 2 == LANES and rows % (4 * PAIRS_PER_BODY) == 0 and rows >= kh + 2
    assert kh == NA_KH and kh % 2 == 0 and PAIRS_PER_BODY % 2 == 0 and SCORE_SLOTS == 4
    assert rel_bias.shape == (H, 2 * kh - 1, 2 * NA_KW - 1) and 2 * NA_KW - 1 <= LANES
    rb = jnp.pad(rel_bias.astype(F32) * LOG2E, ((0, 0), (0, 1), (0, LANES - (2 * NA_KW - 1))))
    blk = (mtot, HEAD_DIM)
    nq, nk = 2 * GRID_W, (kh + 2) * GRID_W
    plane_shape = (3, 2 * kh + 1, GRID_W, 2 * GRID_W)
    nbytes = (8 * mtot * HEAD_DIM * 2 + int(np.prod(plane_shape)) * 4
              + SCORE_SLOTS * nq * (nk + n_ctx) * 4)
    return pl.pallas_call(
        functools.partial(_na_body, n_x=n_x, n_ctx=n_ctx, rows=rows, kh=kh),
        grid=(H,),
        in_specs=[
            pl.BlockSpec(blk, lambda h: (0, h)),
            pl.BlockSpec(blk, lambda h: (0, H + h)),
            pl.BlockSpec(blk, lambda h: (0, 2 * H + h)),
            pl.BlockSpec((1,) + rb.shape[1:], lambda h: (h, 0, 0)),
        ],
        out_specs=pl.BlockSpec(blk, lambda h: (0, h)),
        out_shape=jax.ShapeDtypeStruct((mtot, H * HEAD_DIM), BF16),
        scratch_shapes=[pltpu.VMEM(plane_shape, F32), pltpu.VMEM((SCORE_SLOTS, nq, nk), F32),
                        pltpu.VMEM((SCORE_SLOTS, nq, n_ctx), F32)],
        compiler_params=pltpu.CompilerParams(
            dimension_semantics=("arbitrary",), vmem_limit_bytes=_vmem_limit(nbytes)),
        name="na_attention",
    )(qkv, qkv, qkv, rb)


def _wa_body(q_ref, k_ref, v_ref, ca_ref, sa_ref, cb_ref, sb_ref, band_ref, sink_ref,
             o_ref, kn_ref, sl_ref, sc_ref, *, n_x, n_ctx, group):
    W = WA_WINDOW
    rpb = W // GRID_W
    nblk = n_x // W
    g = pl.program_id(1)
    lane = lax.broadcasted_iota(jnp.int32, (W, HEAD_DIM), 1)
    neg_half = (lane % (HEAD_DIM // 2)) < HEAD_DIM // 4

    def rope(y, b):
        def table(row_ref, col_ref):
            row_part = [jnp.broadcast_to(row_ref[pl.ds(rpb * b + j, 1), :], (GRID_W, HEAD_DIM))
                        for j in range(rpb)]
            return jnp.concatenate(row_part, axis=0) + jnp.concatenate([col_ref[...]] * rpb, axis=0)
        rot = jnp.where(neg_half, -pltpu.roll(y, HEAD_DIM - HEAD_DIM // 4, 1),
                        pltpu.roll(y, HEAD_DIM // 4, 1))
        return y * table(ca_ref, cb_ref) + rot * table(sa_ref, sb_ref)

    @pl.when(g == 0)
    def _prepare_keys():
        def kblock(b, carry):
            r = pl.ds(pl.multiple_of(b * W, W), W)
            kn_ref[r, :] = rope(k_ref[r, :].astype(F32), b).astype(BF16)
            return carry
        lax.fori_loop(0, nblk, kblock, 0, unroll=2)
        kn_ref[n_x:n_x + n_ctx, :] = k_ref[n_x:n_x + n_ctx, :]

    sink = sink_ref[pl.program_id(0) * group + g] * LOG2E
    kc = kn_ref[n_x:n_x + n_ctx, :]
    vc = v_ref[n_x:n_x + n_ctx, :]

    def windows(b):
        a = jnp.clip((b - 1) * W, 0, n_x - 3 * W)
        return pl.ds(pl.multiple_of(b * W, W), W), pl.ds(pl.multiple_of(a, W), 3 * W)

    def scores(b, slot):
        r, krows = windows(b)
        qn = rope(q_ref[r, :].astype(F32), b).astype(BF16)
        s_loc = lax.dot_general(qn, kn_ref[krows, :], _NT, preferred_element_type=F32)
        edge = jnp.where(b == 0, 0, jnp.where(b == nblk - 1, 2, 1))
        sl_ref[slot] = s_loc + band_ref[edge]
        sc_ref[slot] = lax.dot_general(qn, kc, _NT, preferred_element_type=F32)

    def attend(b, slot):
        r, krows = windows(b)
        o = _softmax_pv([(sl_ref[slot], v_ref[krows, :]), (sc_ref[slot], vc)], extra_logit=sink)
        o_ref[r, :] = o.astype(o_ref.dtype)

    scores(0, 0)
    scores(1, 1)

    def pairs(u, carry):
        for ph in range(PAIRS_PER_BODY):
            t = 2 * (PAIRS_PER_BODY * u + ph)
            cur = 2 * (ph % 2)
            for j in range(2):
                scores(jnp.minimum(t + 2 + j, nblk - 1), 2 - cur + j)
                attend(t + j, cur + j)
        return carry

    lax.fori_loop(0, nblk // (2 * PAIRS_PER_BODY), pairs, 0)

    for c in range(n_ctx // W):
        r = slice(n_x + c * W, n_x + (c + 1) * W)
        s = lax.dot_general(q_ref[r, :], kc, _NT, preferred_element_type=F32)
        o_ref[r, :] = _softmax_pv([(s, vc)], extra_logit=sink).astype(o_ref.dtype)


def _rope_tables(rows):
    axis_dim = HEAD_DIM // 2
    inv = ROPE_BASE ** (-jnp.arange(0, axis_dim, 2, dtype=F32) / axis_dim)
    ar = jnp.arange(rows, dtype=F32)[:, None] * inv
    ac = jnp.arange(GRID_W, dtype=F32)[:, None] * inv
    zr = jnp.zeros((rows, axis_dim), F32)
    zc = jnp.zeros((GRID_W, axis_dim), F32)

    def row_tab(f):
        return jnp.concatenate([f(ar), f(ar), zr], axis=-1)

    def col_tab(f):
        return jnp.concatenate([zc, f(ac), f(ac)], axis=-1)

    return row_tab(jnp.cos), row_tab(jnp.sin), col_tab(jnp.cos), col_tab(jnp.sin)


def _wa_band_bias():
    W = WA_WINDOW
    dist = np.arange(3 * W)[None, :] - np.arange(W)[:, None]
    return np.stack([np.where(np.abs(dist + d) <= W, 0.0, NEG_INF) for d in (0, -W, -2 * W)]
                    ).astype(np.float32)


def _wa_attention(qkv, sink, *, n_x, n_ctx, n_heads):
    mtot = qkv.shape[0]
    H, KVH = n_heads, WA_KV_HEADS
    G = H // KVH
    W = WA_WINDOW
    assert W % GRID_W == 0 and n_x % (2 * PAIRS_PER_BODY * W) == 0 and n_ctx % W == 0
    assert W == HEAD_DIM and PAIRS_PER_BODY % 2 == 0 and SCORE_SLOTS == 4
    ca, sa, cb, sb = _rope_tables(n_x // GRID_W)
    band = jnp.asarray(_wa_band_bias())
    blk = (mtot, HEAD_DIM)
    full = lambda arr: pl.BlockSpec(arr.shape, lambda kv, g: (0,) * arr.ndim)
    nbytes = (8 * mtot * HEAD_DIM * 2 + mtot * HEAD_DIM * 2 + 4 * (ca.size + cb.size) * 4
              + 2 * band.size * 4 + SCORE_SLOTS * W * (3 * W + n_ctx) * 4)
    return pl.pallas_call(
        functools.partial(_wa_body, n_x=n_x, n_ctx=n_ctx, group=G),
        grid=(KVH, G),
        in_specs=[
            pl.BlockSpec(blk, lambda kv, g: (0, kv * G + g)),
            pl.BlockSpec(blk, lambda kv, g: (0, H + kv)),
            pl.BlockSpec(blk, lambda kv, g: (0, H + KVH + kv)),
            full(ca), full(sa), full(cb), full(sb), full(band),
            pl.BlockSpec(memory_space=pltpu.SMEM),
        ],
        out_specs=pl.BlockSpec(blk, lambda kv, g: (0, kv * G + g)),
        out_shape=jax.ShapeDtypeStruct((mtot, H * HEAD_DIM), BF16),
        scratch_shapes=[pltpu.VMEM(blk, BF16), pltpu.VMEM((SCORE_SLOTS, W, 3 * W), F32),
                        pltpu.VMEM((SCORE_SLOTS, W, n_ctx), F32)],
        compiler_params=pltpu.CompilerParams(
            dimension_semantics=("arbitrary", "arbitrary"), vmem_limit_bytes=_vmem_limit(nbytes)),
        name="wa_attention",
    )(qkv, qkv, qkv, ca, sa, cb, sb, band, sink.astype(F32))


def _sg_body(u_ref, v_ref, vg_ref, ws_ref, bs_ref, o_ref, *, n_chunks, groups):
    P = SG_CHUNK
    gw = v_ref.shape[1] // groups
    vn = _rms_gain(v_ref[...].astype(F32), vg_ref[...]).astype(BF16)
    for g in range(groups):
        wg = ws_ref[g].astype(BF16)
        cols = slice(g * gw, (g + 1) * gw)
        for c in range(n_chunks):
            rows = slice(c * P, (c + 1) * P)
            sv = jnp.dot(wg, vn[rows, cols], preferred_element_type=F32) + bs_ref[g]
            o_ref[rows, cols] = (u_ref[rows, cols].astype(F32) * sv).astype(o_ref.dtype)


def _spatial_gate(uv, v_gain, w_s, b_s):
    mtot, two_d = uv.shape
    D = two_d // 2
    G, P = SG_GROUPS, SG_CHUNK
    gw = D // G
    assert gw % LANES == 0 and mtot % P == 0
    n_chunks = _largest_divisor(mtot // P, (3, 2, 1))
    bm = n_chunks * P
    bias = jnp.broadcast_to(b_s.astype(F32)[:, :, None], (G, P, gw))
    nbytes = 6 * bm * D * 2 + bm * D * 6 + 2 * (w_s.size + bias.size) * 4
    return pl.pallas_call(
        functools.partial(_sg_body, n_chunks=n_chunks, groups=G),
        grid=(mtot // bm,),
        in_specs=[
            pl.BlockSpec((bm, D), lambda i: (i, 0)),
            pl.BlockSpec((bm, D), lambda i: (i, 1)),
            pl.BlockSpec((1, D), lambda i: (0, 0)),
            pl.BlockSpec(w_s.shape, lambda i: (0, 0, 0)),
            pl.BlockSpec(bias.shape, lambda i: (0, 0, 0)),
        ],
        out_specs=pl.BlockSpec((bm, D), lambda i: (i, 0)),
        out_shape=jax.ShapeDtypeStruct((mtot, D), BF16),
        compiler_params=pltpu.CompilerParams(
            dimension_semantics=("arbitrary",), vmem_limit_bytes=_vmem_limit(nbytes)),
        name="spatial_gate",
    )(uv, uv, v_gain.reshape(1, D), w_s, bias)


def _head_gains(q_gain, k_gain):
    scale = np.float32(HEAD_DIM ** -0.5) * LOG2E
    rows = jnp.stack([q_gain.astype(F32) * scale, k_gain.astype(F32), jnp.ones((HEAD_DIM,), F32)])
    return jnp.concatenate([rows, jnp.zeros((5, HEAD_DIM), F32)], axis=0)


def _ada_modulation(cond, a, b, bias, layer):
    t = _matmul(cond, a, layer, out_dtype=F32, prologue="silu", name="ada_down")
    return _matmul(t, b, layer, out_dtype=F32, epilogue="bias", vec=bias[:, None, :],
                   bn=_largest_divisor(b.shape[2], (2048, 1024, 512, 256, 128)), name="ada_up")


def kernel(x, c, ctx, c_ctx, ada_a, ada_b, ada_bias, norm_mix, norm_mlp, mlp_w1, mlp_w2,
           na_w_qkv, na_w_o, na_q_norm, na_k_norm, na_rel_bias,
           wa_w_qkv, wa_w_o, wa_q_norm, wa_k_norm, wa_sink,
           sg_w_in, sg_v_norm, sg_w_s, sg_b_s, sg_w_out):
    assert x.shape[0] == 1 and ctx.shape[0] == 1 and c.shape[0] == 1
    n_x, D = x.shape[1], x.shape[2]
    n_ctx = ctx.shape[1]
    depth = ada_a.shape[0]
    n_heads = D // HEAD_DIM
    n_mixers = 3

    xc = jnp.concatenate([x[0], ctx[0]], axis=0)
    cond = jnp.zeros((16, D), F32).at[0].set(c[0]).at[1].set(c_ctx)

    for i in range(depth):
        last = i == depth - 1
        kind, j = i % n_mixers, i // n_mixers
        mods = _ada_modulation(cond, ada_a, ada_b, ada_bias, i)
        gate = dict(epilogue="gate_res", vec=mods, n_x=n_x, out_dtype=F32,
                    m_rows=n_x if last else None)

        h = _norm_modulate(xc, norm_mix[i], mods, shift_col=0, scale_col=1, n_x=n_x)
        if kind == 0:
            qkv = _matmul(h, na_w_qkv, j, out_dtype=BF16, epilogue="head_norm",
                          vec=_head_gains(na_q_norm[j], na_k_norm[j]), norm_cols=(D, 2 * D),
                          name="na_qkv")
            o = _na_attention(qkv, na_rel_bias[j], n_x=n_x, n_ctx=n_ctx)
            xc = _matmul(o, na_w_o, j, res=xc, vec_col=2, name="na_out", **gate)
        elif kind == 1:
            qkv = _matmul(h, wa_w_qkv, j, out_dtype=BF16, epilogue="head_norm",
                          vec=_head_gains(wa_q_norm[j], wa_k_norm[j]),
                          norm_cols=(D, D + WA_KV_HEADS * HEAD_DIM), name="wa_qkv")
            o = _wa_attention(qkv, wa_sink[j], n_x=n_x, n_ctx=n_ctx, n_heads=n_heads)
            xc = _matmul(o, wa_w_o, j, res=xc, vec_col=2, name="wa_out", **gate)
        else:
            uv = _matmul(h, sg_w_in, j, out_dtype=BF16, epilogue="gelu", name="sg_in")
            z = _spatial_gate(uv, sg_v_norm[j], sg_w_s[j], sg_b_s[j])
            xc = _matmul(z, sg_w_out, j, res=xc, vec_col=2, name="sg_out", **gate)

        h2 = _norm_modulate(xc, norm_mlp[i], mods, shift_col=3, scale_col=4, n_x=n_x)
        hid = _matmul(h2, mlp_w1, i, out_dtype=BF16, epilogue="relu2", name="mlp_up")
        xc = _matmul(hid, mlp_w2, i, res=xc, vec_col=5, name="mlp_down", **gate)
    return xc[None]
```

```python
import functools

import numpy as np
import jax
import jax.numpy as jnp
from jax import lax
from jax.experimental import pallas as pl
from jax.experimental.pallas import tpu as pltpu

GRID_W = 64
HEAD_DIM = 128
NA_KH = 8
NA_KW = 16
WA_KV_HEADS = 8
WA_WINDOW = 128
ROPE_BASE = 10000.0
SG_GROUPS = 8
SG_CHUNK = 128
N_MOD = 6
EPS = 1e-6
NEG_INF = -1e30
LOG2E = np.float32(np.log2(np.e))
PAIRS_PER_BODY = 8
MIN_CHUNK_ROWS = 128
ROW_CHUNKS = 4
SCORE_SLOTS = 4

LANES = 128
V7X_VMEM_BYTES = 64 * 1024 * 1024
VMEM_CAP = V7X_VMEM_BYTES - 2 * 1024 * 1024
VMEM_SLACK = 2 * 1024 * 1024

F32 = jnp.float32
BF16 = jnp.bfloat16
_NT = (((1,), (1,)), ((), ()))


def _vmem_limit(nbytes):
    return int(min(VMEM_CAP, nbytes + 4 * VMEM_SLACK))


def _row_chunks(bm):
    if bm % (16 * ROW_CHUNKS) or bm // ROW_CHUNKS < MIN_CHUNK_ROWS:
        return (bm,)
    return (bm // ROW_CHUNKS,) * ROW_CHUNKS


def _largest_divisor(n, candidates):
    for c in candidates:
        if n % c == 0:
            return c
    raise ValueError(f"no tile in {candidates} divides {n}")


def _mm_body(*refs, nk, bm, epilogue, prologue, n_x, norm_tiles, row_chunks):
    x_ref, w_ref = refs[0], refs[1]
    pos = 2
    res_ref = vec_ref = None
    if epilogue == "gate_res":
        res_ref, vec_ref = refs[2], refs[3]
        pos = 4
    elif epilogue in ("bias", "head_norm"):
        vec_ref = refs[2]
        pos = 3
    o_ref = refs[pos]
    acc_ref = refs[pos + 1] if nk > 1 else None
    k = pl.program_id(1)
    m = pl.program_id(2)

    if nk > 1:
        rows = pl.ds(pl.multiple_of(m * bm, bm), bm)

        @pl.when(k == 0)
        def _zero():
            acc_ref[rows, :] = jnp.zeros((bm, acc_ref.shape[1]), F32)

    def matmul_rows(r0, nr):
        xv = x_ref[r0:r0 + nr, :]
        if prologue == "silu":
            xv = xv * jax.nn.sigmoid(xv)
        return jnp.dot(xv.astype(BF16), w_ref[...].astype(BF16), preferred_element_type=F32)

    def finish(acc, r0, nr):
        out_rows = slice(r0, r0 + nr)
        if epilogue == "relu2":
            y = jnp.square(jnp.maximum(acc, 0.0))
        elif epilogue == "gelu":
            y = 0.5 * acc * (1.0 + lax.erf(acc * np.float32(np.sqrt(0.5))))
        elif epilogue == "gate_res":
            row_id = m * bm + r0 + lax.broadcasted_iota(jnp.int32, (nr, 1), 0)
            gate = jnp.where(row_id >= n_x, vec_ref[1:2, :], vec_ref[0:1, :])
            y = res_ref[out_rows, :] + gate * acc
        elif epilogue == "bias":
            y = acc + vec_ref[...]
        elif epilogue == "head_norm":
            n = pl.program_id(0)
            kind = (n >= norm_tiles[0]).astype(jnp.int32) + (n >= norm_tiles[1]).astype(jnp.int32)
            gain = vec_ref[pl.ds(kind, 1), :]
            for t in range(acc.shape[1] // HEAD_DIM):
                cols = slice(t * HEAD_DIM, (t + 1) * HEAD_DIM)
                a = acc[:, cols]
                inv_rms = lax.rsqrt(jnp.mean(a * a, axis=-1, keepdims=True) + EPS)
                inv_rms = jnp.where(kind < 2, inv_rms, 1.0)
                o_ref[out_rows, cols] = ((a * inv_rms) * gain).astype(o_ref.dtype)
            return
        else:
            y = acc
        o_ref[out_rows, :] = y.astype(o_ref.dtype)

    if nk == 1:
        r0 = 0
        for nr in row_chunks:
            finish(matmul_rows(r0, nr), r0, nr)
            r0 += nr
    else:
        acc_ref[rows, :] = acc_ref[rows, :] + matmul_rows(0, bm)

        @pl.when(k == nk - 1)
        def _emit():
            finish(acc_ref[rows, :], 0, bm)


def _matmul(x, w, layer, *, out_dtype, epilogue=None, prologue=None, res=None, vec=None, vec_col=0,
            n_x=0, m_rows=None, norm_cols=None, bm=None, bn=None, bk=None, name="mm"):
    M = x.shape[0] if m_rows is None else m_rows
    _, K, N = w.shape
    bk = bk or _largest_divisor(K, (4096, 2048, 1024, 512, 256, 128))
    nk = K // bk
    last_k = nk - 1
    out_bytes = jnp.dtype(out_dtype).itemsize

    def vmem_bytes(bm_, bn_):
        n = 2 * bm_ * bk * x.dtype.itemsize + 2 * bk * bn_ * 4 + 2 * bm_ * bn_ * out_bytes
        if epilogue == "gate_res":
            n += 2 * bm_ * bn_ * 4 + 2 * vec.shape[0] * bn_ * 4
        if nk > 1:
            n += M * bn_ * 4
        return n

    def best_bm(bn_):
        return next(c for c in (2048, 1408, 1056, 1024, 768, 512, 384, 256, 128, 16)
                    if M % c == 0 and vmem_bytes(c, bn_) <= VMEM_CAP - VMEM_SLACK)

    if bn is None:
        bn = _largest_divisor(int(np.gcd.reduce((N,) + tuple(norm_cols or ()))), (512, 256, 128))
        if (epilogue == "gate_res" and nk == 1 and bm is None and N % (2 * bn) == 0
                and 2 * best_bm(2 * bn) > best_bm(bn)):
            bn *= 2
    if bm is None:
        bm = best_bm(bn)

    def out_rows(k, m):
        return m if nk == 1 else jnp.where(k == last_k, m, 0)

    in_specs = [
        pl.BlockSpec((bm, bk), lambda n, k, m: (m, k)),
        pl.BlockSpec((None, bk, bn), lambda n, k, m: (layer, k, n)),
    ]
    args = [x, w]
    if epilogue == "gate_res":
        in_specs.append(pl.BlockSpec((bm, bn), lambda n, k, m: (out_rows(k, m), n)))
        in_specs.append(pl.BlockSpec((vec.shape[0], bn), lambda n, k, m: (0, vec_col * (N // bn) + n)))
        args += [res, vec]
    elif epilogue == "bias":
        in_specs.append(pl.BlockSpec((None, 1, bn), lambda n, k, m: (layer, 0, n)))
        args.append(vec)
    norm_tiles = None
    if epilogue == "head_norm":
        in_specs.append(pl.BlockSpec(vec.shape, lambda n, k, m: (0, 0)))
        args.append(vec)
        assert norm_cols[0] % bn == 0 and norm_cols[1] % bn == 0 and bn % HEAD_DIM == 0
        norm_tiles = (norm_cols[0] // bn, norm_cols[1] // bn)
    scratch = [pltpu.VMEM((M, bn), F32)] if nk > 1 else []
    row_chunks = _row_chunks(bm)
    body = functools.partial(_mm_body, nk=nk, bm=bm, epilogue=epilogue, prologue=prologue, n_x=n_x,
                             norm_tiles=norm_tiles, row_chunks=row_chunks)
    return pl.pallas_call(
        body,
        grid=(N // bn, nk, M // bm),
        in_specs=in_specs,
        out_specs=pl.BlockSpec((bm, bn), lambda n, k, m: (out_rows(k, m), n)),
        out_shape=jax.ShapeDtypeStruct((M, N), out_dtype),
        scratch_shapes=scratch,
        compiler_params=pltpu.CompilerParams(
            dimension_semantics=("arbitrary", "arbitrary", "arbitrary"),
            vmem_limit_bytes=_vmem_limit(vmem_bytes(bm, bn))),
        name=name,
    )(*args)


def _norm_body(x_ref, g_ref, sh_ref, sc_ref, o_ref, *, n_x):
    bm = x_ref.shape[0]
    chunk = 8
    assert n_x % chunk == 0

    def rows_chunk(c, carry):
        r = pl.ds(pl.multiple_of(c * chunk, chunk), chunk)
        t = (pl.program_id(0) * bm + c * chunk >= n_x).astype(jnp.int32)
        x = x_ref[r, :]
        y = x * lax.rsqrt(jnp.mean(x * x, axis=-1, keepdims=True) + EPS)
        y = y * g_ref[...]
        o_ref[r, :] = (y * (1.0 + sc_ref[pl.ds(t, 1), :]) + sh_ref[pl.ds(t, 1), :]).astype(o_ref.dtype)
        return carry

    lax.fori_loop(0, bm // chunk, rows_chunk, 0, unroll=4)


def _norm_modulate(x, gain, mods, *, shift_col, scale_col, n_x):
    M, D = x.shape
    bm = _largest_divisor(M, (1024, 768, 512, 384, 256, 128, 64, 32, 16))
    nbytes = 2 * bm * D * 4 + 2 * bm * D * 2 + 6 * 8 * D * 4
    return pl.pallas_call(
        functools.partial(_norm_body, n_x=n_x),
        grid=(M // bm,),
        in_specs=[
            pl.BlockSpec((bm, D), lambda i: (i, 0)),
            pl.BlockSpec((1, D), lambda i: (0, 0)),
            pl.BlockSpec((mods.shape[0], D), lambda i: (0, shift_col)),
            pl.BlockSpec((mods.shape[0], D), lambda i: (0, scale_col)),
        ],
        out_specs=pl.BlockSpec((bm, D), lambda i: (i, 0)),
        out_shape=jax.ShapeDtypeStruct((M, D), BF16),
        compiler_params=pltpu.CompilerParams(
            dimension_semantics=("arbitrary",), vmem_limit_bytes=_vmem_limit(nbytes)),
        name="norm_modulate",
    )(x, gain.reshape(1, D), mods, mods)


def _rms_gain(xf, gain):
    return (xf * lax.rsqrt(jnp.mean(xf * xf, axis=-1, keepdims=True) + EPS)) * gain


def _softmax_pv(parts, extra_logit=None):
    m_tile = None
    for s, _ in parts:
        for t in range(s.shape[1] // LANES):
            st = s[:, t * LANES:(t + 1) * LANES]
            m_tile = st if m_tile is None else jnp.maximum(m_tile, st)
    m = jnp.max(m_tile, axis=-1, keepdims=True)
    if extra_logit is not None:
        m = jnp.maximum(m, extra_logit)
    acc = None
    for s, v in parts:
        v_ones = jnp.concatenate([v, jnp.ones_like(v)], axis=1)
        ov = jnp.dot(jnp.exp2(s - m).astype(BF16), v_ones, preferred_element_type=F32)
        acc = ov if acc is None else acc + ov
    d = acc.shape[1] // 2
    o, l = acc[:, :d], acc[:, d:]
    if extra_logit is not None:
        l = l + jnp.exp2(extra_logit - m)
    return o / l


def _na_bias_planes(rb_ref, tb_ref, kh):
    W, kw = GRID_W, NA_KW
    shape = (W, 2 * W)
    qc = lax.broadcasted_iota(jnp.int32, shape, 0)
    lane = lax.broadcasted_iota(jnp.int32, shape, 1)
    kc = lane % W
    win = jnp.clip(qc - kw // 2, 0, W - kw)
    in_window = (kc >= win) & (kc < win + kw)
    left = lane < W
    neg = jnp.full(shape, NEG_INF, F32)

    def toeplitz(d, lane0):
        if d < 0 or d > 2 * kh - 2:
            return neg
        row = jnp.broadcast_to(rb_ref[0, d:d + 1, :], shape)
        return pltpu.roll(row, (lane0 - (kw - 1)) % LANES, 1, stride=1, stride_axis=0)

    for p in range(2 * kh + 1):
        both = jnp.where(in_window, jnp.where(left, toeplitz(p - 1, 0), toeplitz(p, W)), NEG_INF)
        tb_ref[0, p] = both
        tb_ref[1, p] = jnp.where(left, both, NEG_INF)
        tb_ref[2, p] = jnp.where(left, NEG_INF, both)


def _na_body(q_ref, k_ref, v_ref, rb_ref, o_ref, tb_ref, sl_ref, sc_ref, *, n_x, n_ctx, rows, kh):
    W = GRID_W
    nkr = kh + 2
    n_steps = rows // 2
    kc = k_ref[n_x:n_x + n_ctx, :]
    vc = v_ref[n_x:n_x + n_ctx, :]
    _na_bias_planes(rb_ref, tb_ref, kh)

    def windows(i):
        r = 2 * i
        a = jnp.clip(r - kh // 2, 0, rows - nkr)
        return (r, a, pl.ds(pl.multiple_of(r * W, 2 * W), 2 * W),
                pl.ds(pl.multiple_of(a * W, 2 * W), nkr * W))

    def scores(i, slot):
        r, a, qrows, krows = windows(i)
        q2 = q_ref[qrows, :]
        s_loc = lax.dot_general(q2, k_ref[krows, :], _NT, preferred_element_type=F32)
        bias_rows = []
        for j in range(2):
            qr = r + j
            r0 = jnp.clip(qr - kh // 2, 0, rows - kh)
            pieces = []
            for t in range(nkr // 2):
                kr0 = a + 2 * t
                in0 = (kr0 >= r0) & (kr0 < r0 + kh)
                in1 = (kr0 + 1 >= r0) & (kr0 + 1 < r0 + kh)
                variant = jnp.where(in0 & in1, 0, jnp.where(in0, 1, 2))
                plane = jnp.where(in0 | in1, jnp.clip(kr0 - qr + kh, 0, 2 * kh - 1), 2 * kh)
                pieces.append(tb_ref[variant, plane])
            bias_rows.append(jnp.concatenate(pieces, axis=1))
        sl_ref[slot] = s_loc + jnp.concatenate(bias_rows, axis=0)
        sc_ref[slot] = lax.dot_general(q2, kc, _NT, preferred_element_type=F32)

    def attend(i, slot):
        _, _, qrows, krows = windows(i)
        o = _softmax_pv([(sl_ref[slot], v_ref[krows, :]), (sc_ref[slot], vc)])
        o_ref[qrows, :] = o.astype(o_ref.dtype)

    scores(0, 0)
    scores(1, 1)

    def pairs(u, carry):
        for ph in range(PAIRS_PER_BODY):
            t = 2 * (PAIRS_PER_BODY * u + ph)
            cur = 2 * (ph % 2)
            for j in range(2):
                scores(jnp.minimum(t + 2 + j, n_steps - 1), 2 - cur + j)
                attend(t + j, cur + j)
        return carry

    lax.fori_loop(0, n_steps // (2 * PAIRS_PER_BODY), pairs, 0)

    ctx_rows = slice(n_x, n_x + n_ctx)
    s = lax.dot_general(q_ref[ctx_rows, :], kc, _NT, preferred_element_type=F32)
    o_ref[ctx_rows, :] = _softmax_pv([(s, vc)]).astype(o_ref.dtype)

def _na_attention(qkv, rel_bias, *, n_x, n_ctx):
    mtot, width = qkv.shape
    H = width // (3 * HEAD_DIM)
    rows = n_x // GRID_W
    kh = min(NA_KH, rows)
    assert GRID_W * 2 == LANES and rows % (4 * PAIRS_PER_BODY) == 0 and rows >= kh + 2
    assert kh == NA_KH and kh % 2 == 0 and PAIRS_PER_BODY % 2 == 0 and SCORE_SLOTS == 4
    assert rel_bias.shape == (H, 2 * kh - 1, 2 * NA_KW - 1) and 2 * NA_KW - 1 <= LANES
    rb = jnp.pad(rel_bias.astype(F32) * LOG2E, ((0, 0), (0, 1), (0, LANES - (2 * NA_KW - 1))))
    blk = (mtot, HEAD_DIM)
    nq, nk = 2 * GRID_W, (kh + 2) * GRID_W
    plane_shape = (3, 2 * kh + 1, GRID_W, 2 * GRID_W)
    nbytes = (8 * mtot * HEAD_DIM * 2 + int(np.prod(plane_shape)) * 4
              + SCORE_SLOTS * nq * (nk + n_ctx) * 4)
    return pl.pallas_call(
        functools.partial(_na_body, n_x=n_x, n_ctx=n_ctx, rows=rows, kh=kh),
        grid=(H,),
        in_specs=[
            pl.BlockSpec(blk, lambda h: (0, h)),
            pl.BlockSpec(blk, lambda h: (0, H + h)),
            pl.BlockSpec(blk, lambda h: (0, 2 * H + h)),
            pl.BlockSpec((1,) + rb.shape[1:], lambda h: (h, 0, 0)),
        ],
        out_specs=pl.BlockSpec(blk, lambda h: (0, h)),
        out_shape=jax.ShapeDtypeStruct((mtot, H * HEAD_DIM), BF16),
        scratch_shapes=[pltpu.VMEM(plane_shape, F32), pltpu.VMEM((SCORE_SLOTS, nq, nk), F32),
                        pltpu.VMEM((SCORE_SLOTS, nq, n_ctx), F32)],
        compiler_params=pltpu.CompilerParams(
            dimension_semantics=("arbitrary",), vmem_limit_bytes=_vmem_limit(nbytes)),
        name="na_attention",
    )(qkv, qkv, qkv, rb)


def _wa_body(q_ref, k_ref, v_ref, ca_ref, sa_ref, cb_ref, sb_ref, band_ref, sink_ref,
             o_ref, kn_ref, sl_ref, sc_ref, *, n_x, n_ctx, group):
    W = WA_WINDOW
    rpb = W // GRID_W
    nblk = n_x // W
    g = pl.program_id(1)
    lane = lax.broadcasted_iota(jnp.int32, (W, HEAD_DIM), 1)
    neg_half = (lane % (HEAD_DIM // 2)) < HEAD_DIM // 4

    def rope(y, b):
        def table(row_ref, col_ref):
            row_part = [jnp.broadcast_to(row_ref[pl.ds(rpb * b + j, 1), :], (GRID_W, HEAD_DIM))
                        for j in range(rpb)]
            return jnp.concatenate(row_part, axis=0) + jnp.concatenate([col_ref[...]] * rpb, axis=0)
        rot = jnp.where(neg_half, -pltpu.roll(y, HEAD_DIM - HEAD_DIM // 4, 1),
                        pltpu.roll(y, HEAD_DIM // 4, 1))
        return y * table(ca_ref, cb_ref) + rot * table(sa_ref, sb_ref)

    @pl.when(g == 0)
    def _prepare_keys():
        def kblock(b, carry):
            r = pl.ds(pl.multiple_of(b * W, W), W)
            kn_ref[r, :] = rope(k_ref[r, :].astype(F32), b).astype(BF16)
            return carry
        lax.fori_loop(0, nblk, kblock, 0, unroll=2)
        kn_ref[n_x:n_x + n_ctx, :] = k_ref[n_x:n_x + n_ctx, :]

    sink = sink_ref[pl.program_id(0) * group + g] * LOG2E
    kc = kn_ref[n_x:n_x + n_ctx, :]
    vc = v_ref[n_x:n_x + n_ctx, :]

    def windows(b):
        a = jnp.clip((b - 1) * W, 0, n_x - 3 * W)
        return pl.ds(pl.multiple_of(b * W, W), W), pl.ds(pl.multiple_of(a, W), 3 * W)

    def scores(b, slot):
        r, krows = windows(b)
        qn = rope(q_ref[r, :].astype(F32), b).astype(BF16)
        s_loc = lax.dot_general(qn, kn_ref[krows, :], _NT, preferred_element_type=F32)
        edge = jnp.where(b == 0, 0, jnp.where(b == nblk - 1, 2, 1))
        sl_ref[slot] = s_loc + band_ref[edge]
        sc_ref[slot] = lax.dot_general(qn, kc, _NT, preferred_element_type=F32)

    def attend(b, slot):
        r, krows = windows(b)
        o = _softmax_pv([(sl_ref[slot], v_ref[krows, :]), (sc_ref[slot], vc)], extra_logit=sink)
        o_ref[r, :] = o.astype(o_ref.dtype)

    scores(0, 0)
    scores(1, 1)

    def pairs(u, carry):
        for ph in range(PAIRS_PER_BODY):
            t = 2 * (PAIRS_PER_BODY * u + ph)
            cur = 2 * (ph % 2)
            for j in range(2):
                scores(jnp.minimum(t + 2 + j, nblk - 1), 2 - cur + j)
                attend(t + j, cur + j)
        return carry

    lax.fori_loop(0, nblk // (2 * PAIRS_PER_BODY), pairs, 0)

    for c in range(n_ctx // W):
        r = slice(n_x + c * W, n_x + (c + 1) * W)
        s = lax.dot_general(q_ref[r, :], kc, _NT, preferred_element_type=F32)
        o_ref[r, :] = _softmax_pv([(s, vc)], extra_logit=sink).astype(o_ref.dtype)


def _rope_tables(rows):
    axis_dim = HEAD_DIM // 2
    inv = ROPE_BASE ** (-jnp.arange(0, axis_dim, 2, dtype=F32) / axis_dim)
    ar = jnp.arange(rows, dtype=F32)[:, None] * inv
    ac = jnp.arange(GRID_W, dtype=F32)[:, None] * inv
    zr = jnp.zeros((rows, axis_dim), F32)
    zc = jnp.zeros((GRID_W, axis_dim), F32)

    def row_tab(f):
        return jnp.concatenate([f(ar), f(ar), zr], axis=-1)

    def col_tab(f):
        return jnp.concatenate([zc, f(ac), f(ac)], axis=-1)

    return row_tab(jnp.cos), row_tab(jnp.sin), col_tab(jnp.cos), col_tab(jnp.sin)


def _wa_band_bias():
    W = WA_WINDOW
    dist = np.arange(3 * W)[None, :] - np.arange(W)[:, None]
    return np.stack([np.where(np.abs(dist + d) <= W, 0.0, NEG_INF) for d in (0, -W, -2 * W)]
                    ).astype(np.float32)


def _wa_attention(qkv, sink, *, n_x, n_ctx, n_heads):
    mtot = qkv.shape[0]
    H, KVH = n_heads, WA_KV_HEADS
    G = H // KVH
    W = WA_WINDOW
    assert W % GRID_W == 0 and n_x % (2 * PAIRS_PER_BODY * W) == 0 and n_ctx % W == 0
    assert W == HEAD_DIM and PAIRS_PER_BODY % 2 == 0 and SCORE_SLOTS == 4
    ca, sa, cb, sb = _rope_tables(n_x // GRID_W)
    band = jnp.asarray(_wa_band_bias())
    blk = (mtot, HEAD_DIM)
    full = lambda arr: pl.BlockSpec(arr.shape, lambda kv, g: (0,) * arr.ndim)
    nbytes = (8 * mtot * HEAD_DIM * 2 + mtot * HEAD_DIM * 2 + 4 * (ca.size + cb.size) * 4
              + 2 * band.size * 4 + SCORE_SLOTS * W * (3 * W + n_ctx) * 4)
    return pl.pallas_call(
        functools.partial(_wa_body, n_x=n_x, n_ctx=n_ctx, group=G),
        grid=(KVH, G),
        in_specs=[
            pl.BlockSpec(blk, lambda kv, g: (0, kv * G + g)),
            pl.BlockSpec(blk, lambda kv, g: (0, H + kv)),
            pl.BlockSpec(blk, lambda kv, g: (0, H + KVH + kv)),
            full(ca), full(sa), full(cb), full(sb), full(band),
            pl.BlockSpec(memory_space=pltpu.SMEM),
        ],
        out_specs=pl.BlockSpec(blk, lambda kv, g: (0, kv * G + g)),
        out_shape=jax.ShapeDtypeStruct((mtot, H * HEAD_DIM), BF16),
        scratch_shapes=[pltpu.VMEM(blk, BF16), pltpu.VMEM((SCORE_SLOTS, W, 3 * W), F32),
                        pltpu.VMEM((SCORE_SLOTS, W, n_ctx), F32)],
        compiler_params=pltpu.CompilerParams(
            dimension_semantics=("arbitrary", "arbitrary"), vmem_limit_bytes=_vmem_limit(nbytes)),
        name="wa_attention",
    )(qkv, qkv, qkv, ca, sa, cb, sb, band, sink.astype(F32))


def _sg_body(u_ref, v_ref, vg_ref, ws_ref, bs_ref, o_ref, *, n_chunks, groups):
    P = SG_CHUNK
    gw = v_ref.shape[1] // groups
    vn = _rms_gain(v_ref[...].astype(F32), vg_ref[...]).astype(BF16)
    for g in range(groups):
        wg = ws_ref[g].astype(BF16)
        cols = slice(g * gw, (g + 1) * gw)
        for c in range(n_chunks):
            rows = slice(c * P, (c + 1) * P)
            sv = jnp.dot(wg, vn[rows, cols], preferred_element_type=F32) + bs_ref[g]
            o_ref[rows, cols] = (u_ref[rows, cols].astype(F32) * sv).astype(o_ref.dtype)


def _spatial_gate(uv, v_gain, w_s, b_s):
    mtot, two_d = uv.shape
    D = two_d // 2
    G, P = SG_GROUPS, SG_CHUNK
    gw = D // G
    assert gw % LANES == 0 and mtot % P == 0
    n_chunks = _largest_divisor(mtot // P, (3, 2, 1))
    bm = n_chunks * P
    bias = jnp.broadcast_to(b_s.astype(F32)[:, :, None], (G, P, gw))
    nbytes = 6 * bm * D * 2 + bm * D * 6 + 2 * (w_s.size + bias.size) * 4
    return pl.pallas_call(
        functools.partial(_sg_body, n_chunks=n_chunks, groups=G),
        grid=(mtot // bm,),
        in_specs=[
            pl.BlockSpec((bm, D), lambda i: (i, 0)),
            pl.BlockSpec((bm, D), lambda i: (i, 1)),
            pl.BlockSpec((1, D), lambda i: (0, 0)),
            pl.BlockSpec(w_s.shape, lambda i: (0, 0, 0)),
            pl.BlockSpec(bias.shape, lambda i: (0, 0, 0)),
        ],
        out_specs=pl.BlockSpec((bm, D), lambda i: (i, 0)),
        out_shape=jax.ShapeDtypeStruct((mtot, D), BF16),
        compiler_params=pltpu.CompilerParams(
            dimension_semantics=("arbitrary",), vmem_limit_bytes=_vmem_limit(nbytes)),
        name="spatial_gate",
    )(uv, uv, v_gain.reshape(1, D), w_s, bias)


def _head_gains(q_gain, k_gain):
    scale = np.float32(HEAD_DIM ** -0.5) * LOG2E
    rows = jnp.stack([q_gain.astype(F32) * scale, k_gain.astype(F32), jnp.ones((HEAD_DIM,), F32)])
    return jnp.concatenate([rows, jnp.zeros((5, HEAD_DIM), F32)], axis=0)


def _ada_modulation(cond, a, b, bias, layer):
    t = _matmul(cond, a, layer, out_dtype=F32, prologue="silu", name="ada_down")
    return _matmul(t, b, layer, out_dtype=F32, epilogue="bias", vec=bias[:, None, :],
                   bn=_largest_divisor(b.shape[2], (2048, 1024, 512, 256, 128)), name="ada_up")


def kernel(x, c, ctx, c_ctx, ada_a, ada_b, ada_bias, norm_mix, norm_mlp, mlp_w1, mlp_w2,
           na_w_qkv, na_w_o, na_q_norm, na_k_norm, na_rel_bias,
           wa_w_qkv, wa_w_o, wa_q_norm, wa_k_norm, wa_sink,
           sg_w_in, sg_v_norm, sg_w_s, sg_b_s, sg_w_out):
    assert x.shape[0] == 1 and ctx.shape[0] == 1 and c.shape[0] == 1
    n_x, D = x.shape[1], x.shape[2]
    n_ctx = ctx.shape[1]
    depth = ada_a.shape[0]
    n_heads = D // HEAD_DIM
    n_mixers = 3

    xc = jnp.concatenate([x[0], ctx[0]], axis=0)
    cond = jnp.zeros((16, D), F32).at[0].set(c[0]).at[1].set(c_ctx)

    for i in range(depth):
        last = i == depth - 1
        kind, j = i % n_mixers, i // n_mixers
        mods = _ada_modulation(cond, ada_a, ada_b, ada_bias, i)
        gate = dict(epilogue="gate_res", vec=mods, n_x=n_x, out_dtype=F32,
                    m_rows=n_x if last else None)

        h = _norm_modulate(xc, norm_mix[i], mods, shift_col=0, scale_col=1, n_x=n_x)
        if kind == 0:
            qkv = _matmul(h, na_w_qkv, j, out_dtype=BF16, epilogue="head_norm",
                          vec=_head_gains(na_q_norm[j], na_k_norm[j]), norm_cols=(D, 2 * D),
                          name="na_qkv")
            o = _na_attention(qkv, na_rel_bias[j], n_x=n_x, n_ctx=n_ctx)
            xc = _matmul(o, na_w_o, j, res=xc, vec_col=2, name="na_out", **gate)
        elif kind == 1:
            qkv = _matmul(h, wa_w_qkv, j, out_dtype=BF16, epilogue="head_norm",
                          vec=_head_gains(wa_q_norm[j], wa_k_norm[j]),
                          norm_cols=(D, D + WA_KV_HEADS * HEAD_DIM), name="wa_qkv")
            o = _wa_attention(qkv, wa_sink[j], n_x=n_x, n_ctx=n_ctx, n_heads=n_heads)
            xc = _matmul(o, wa_w_o, j, res=xc, vec_col=2, name="wa_out", **gate)
        else:
            uv = _matmul(h, sg_w_in, j, out_dtype=BF16, epilogue="gelu", name="sg_in")
            z = _spatial_gate(uv, sg_v_norm[j], sg_w_s[j], sg_b_s[j])
            xc = _matmul(z, sg_w_out, j, res=xc, vec_col=2, name="sg_out", **gate)

        h2 = _norm_modulate(xc, norm_mlp[i], mods, shift_col=3, scale_col=4, n_x=n_x)
        hid = _matmul(h2, mlp_w1, i, out_dtype=BF16, epilogue="relu2", name="mlp_up")
        xc = _matmul(hid, mlp_w2, i, res=xc, vec_col=5, name="mlp_down", **gate)
    return xc[None]
```

```python
import functools

import numpy as np
import jax
import jax.numpy as jnp
from jax import lax
from jax.experimental import pallas as pl
from jax.experimental.pallas import tpu as pltpu

GRID_W = 64
HEAD_DIM = 128
NA_KH = 8
NA_KW = 16
WA_KV_HEADS = 8
WA_WINDOW = 128
ROPE_BASE = 10000.0
SG_GROUPS = 8
SG_CHUNK = 128
N_MOD = 6
EPS = 1e-6
NEG_INF = -1e30
LOG2E = np.float32(np.log2(np.e))
PAIRS_PER_BODY = 8
MIN_CHUNK_ROWS = 128
ROW_CHUNKS = 4
SCORE_SLOTS = 4

LANES = 128
V7X_VMEM_BYTES = 64 * 1024 * 1024
VMEM_CAP = V7X_VMEM_BYTES - 2 * 1024 * 1024
VMEM_SLACK = 2 * 1024 * 1024

F32 = jnp.float32
BF16 = jnp.bfloat16
_NT = (((1,), (1,)), ((), ()))


def _vmem_limit(nbytes):
    return int(min(VMEM_CAP, nbytes + 4 * VMEM_SLACK))


def _row_chunks(bm):
    if bm % (16 * ROW_CHUNKS) or bm // ROW_CHUNKS < MIN_CHUNK_ROWS:
        return (bm,)
    return (bm // ROW_CHUNKS,) * ROW_CHUNKS


def _largest_divisor(n, candidates):
    for c in candidates:
        if n % c == 0:
            return c
    raise ValueError(f"no tile in {candidates} divides {n}")


def _mm_body(*refs, nk, bm, epilogue, prologue, n_x, norm_tiles, row_chunks):
    x_ref, w_ref = refs[0], refs[1]
    pos = 2
    res_ref = vec_ref = None
    if epilogue == "gate_res":
        res_ref, vec_ref = refs[2], refs[3]
        pos = 4
    elif epilogue in ("bias", "head_norm"):
        vec_ref = refs[2]
        pos = 3
    o_ref = refs[pos]
    acc_ref = refs[pos + 1] if nk > 1 else None
    k = pl.program_id(1)
    m = pl.program_id(2)

    if nk > 1:
        rows = pl.ds(pl.multiple_of(m * bm, bm), bm)

        @pl.when(k == 0)
        def _zero():
            acc_ref[rows, :] = jnp.zeros((bm, acc_ref.shape[1]), F32)

    def matmul_rows(r0, nr):
        xv = x_ref[r0:r0 + nr, :]
        if prologue == "silu":
            xv = xv * jax.nn.sigmoid(xv)
        return jnp.dot(xv.astype(BF16), w_ref[...].astype(BF16), preferred_element_type=F32)

    def finish(acc, r0, nr):
        out_rows = slice(r0, r0 + nr)
        if epilogue == "relu2":
            y = jnp.square(jnp.maximum(acc, 0.0))
        elif epilogue == "gelu":
            y = 0.5 * acc * (1.0 + lax.erf(acc * np.float32(np.sqrt(0.5))))
        elif epilogue == "gate_res":
            row_id = m * bm + r0 + lax.broadcasted_iota(jnp.int32, (nr, 1), 0)
            gate = jnp.where(row_id >= n_x, vec_ref[1:2, :], vec_ref[0:1, :])
            y = res_ref[out_rows, :] + gate * acc
        elif epilogue == "bias":
            y = acc + vec_ref[...]
        elif epilogue == "head_norm":
            n = pl.program_id(0)
            kind = (n >= norm_tiles[0]).astype(jnp.int32) + (n >= norm_tiles[1]).astype(jnp.int32)
            gain = vec_ref[pl.ds(kind, 1), :]
            for t in range(acc.shape[1] // HEAD_DIM):
                cols = slice(t * HEAD_DIM, (t + 1) * HEAD_DIM)
                a = acc[:, cols]
                inv_rms = lax.rsqrt(jnp.mean(a * a, axis=-1, keepdims=True) + EPS)
                inv_rms = jnp.where(kind < 2, inv_rms, 1.0)
                o_ref[out_rows, cols] = ((a * inv_rms) * gain).astype(o_ref.dtype)
            return
        else:
            y = acc
        o_ref[out_rows, :] = y.astype(o_ref.dtype)

    if nk == 1:
        r0 = 0
        for nr in row_chunks:
            finish(matmul_rows(r0, nr), r0, nr)
            r0 += nr
    else:
        acc_ref[rows, :] = acc_ref[rows, :] + matmul_rows(0, bm)

        @pl.when(k == nk - 1)
        def _emit():
            finish(acc_ref[rows, :], 0, bm)


def _matmul(x, w, layer, *, out_dtype, epilogue=None, prologue=None, res=None, vec=None, vec_col=0,
            n_x=0, m_rows=None, norm_cols=None, bm=None, bn=None, bk=None, name="mm"):
    M = x.shape[0] if m_rows is None else m_rows
    _, K, N = w.shape
    bk = bk or _largest_divisor(K, (4096, 2048, 1024, 512, 256, 128))
    nk = K // bk
    last_k = nk - 1
    out_bytes = jnp.dtype(out_dtype).itemsize

    def vmem_bytes(bm_, bn_):
        n = 2 * bm_ * bk * x.dtype.itemsize + 2 * bk * bn_ * 4 + 2 * bm_ * bn_ * out_bytes
        if epilogue == "gate_res":
            n += 2 * bm_ * bn_ * 4 + 2 * vec.shape[0] * bn_ * 4
        if nk > 1:
            n += M * bn_ * 4
        return n

    def best_bm(bn_):
        return next(c for c in (2112, 2048, 1408, 1056, 1024, 768, 512, 384, 256, 128, 16)
                    if M % c == 0 and vmem_bytes(c, bn_) <= VMEM_CAP - VMEM_SLACK)

    if bn is None:
        col_unit = int(np.gcd.reduce((N,) + tuple(norm_cols or ())))
        bn = _largest_divisor(col_unit, (512, 256, 128))
        if (nk == 1 and bm is None and epilogue == "gate_res" and col_unit % (2 * bn) == 0
                and 2 * best_bm(2 * bn) > best_bm(bn)):
            bn *= 2
    if bm is None:
        bm = best_bm(bn)

    def out_rows(k, m):
        return m if nk == 1 else jnp.where(k == last_k, m, 0)

    in_specs = [
        pl.BlockSpec((bm, bk), lambda n, k, m: (m, k)),
        pl.BlockSpec((None, bk, bn), lambda n, k, m: (layer, k, n)),
    ]
    args = [x, w]
    if epilogue == "gate_res":
        in_specs.append(pl.BlockSpec((bm, bn), lambda n, k, m: (out_rows(k, m), n)))
        in_specs.append(pl.BlockSpec((vec.shape[0], bn), lambda n, k, m: (0, vec_col * (N // bn) + n)))
        args += [res, vec]
    elif epilogue == "bias":
        in_specs.append(pl.BlockSpec((None, 1, bn), lambda n, k, m: (layer, 0, n)))
        args.append(vec)
    norm_tiles = None
    if epilogue == "head_norm":
        in_specs.append(pl.BlockSpec(vec.shape, lambda n, k, m: (0, 0)))
        args.append(vec)
        assert norm_cols[0] % bn == 0 and norm_cols[1] % bn == 0 and bn % HEAD_DIM == 0
        norm_tiles = (norm_cols[0] // bn, norm_cols[1] // bn)
    scratch = [pltpu.VMEM((M, bn), F32)] if nk > 1 else []
    row_chunks = _row_chunks(bm)
    body = functools.partial(_mm_body, nk=nk, bm=bm, epilogue=epilogue, prologue=prologue, n_x=n_x,
                             norm_tiles=norm_tiles, row_chunks=row_chunks)
    return pl.pallas_call(
        body,
        grid=(N // bn, nk, M // bm),
        in_specs=in_specs,
        out_specs=pl.BlockSpec((bm, bn), lambda n, k, m: (out_rows(k, m), n)),
        out_shape=jax.ShapeDtypeStruct((M, N), out_dtype),
        scratch_shapes=scratch,
        compiler_params=pltpu.CompilerParams(
            dimension_semantics=("arbitrary", "arbitrary", "arbitrary"),
            vmem_limit_bytes=_vmem_limit(vmem_bytes(bm, bn))),
        name=name,
    )(*args)


def _norm_body(x_ref, g_ref, sh_ref, sc_ref, o_ref, *, n_x):
    bm = x_ref.shape[0]
    chunk = 8
    assert n_x % chunk == 0

    def rows_chunk(c, carry):
        r = pl.ds(pl.multiple_of(c * chunk, chunk), chunk)
        t = (pl.program_id(0) * bm + c * chunk >= n_x).astype(jnp.int32)
        x = x_ref[r, :]
        y = x * lax.rsqrt(jnp.mean(x * x, axis=-1, keepdims=True) + EPS)
        y = y * g_ref[...]
        o_ref[r, :] = (y * (1.0 + sc_ref[pl.ds(t, 1), :]) + sh_ref[pl.ds(t, 1), :]).astype(o_ref.dtype)
        return carry

    lax.fori_loop(0, bm // chunk, rows_chunk, 0, unroll=4)


def _norm_modulate(x, gain, mods, *, shift_col, scale_col, n_x):
    M, D = x.shape
    bm = _largest_divisor(M, (1024, 768, 512, 384, 256, 128, 64, 32, 16))
    nbytes = 2 * bm * D * 4 + 2 * bm * D * 2 + 6 * 8 * D * 4
    return pl.pallas_call(
        functools.partial(_norm_body, n_x=n_x),
        grid=(M // bm,),
        in_specs=[
            pl.BlockSpec((bm, D), lambda i: (i, 0)),
            pl.BlockSpec((1, D), lambda i: (0, 0)),
            pl.BlockSpec((mods.shape[0], D), lambda i: (0, shift_col)),
            pl.BlockSpec((mods.shape[0], D), lambda i: (0, scale_col)),
        ],
        out_specs=pl.BlockSpec((bm, D), lambda i: (i, 0)),
        out_shape=jax.ShapeDtypeStruct((M, D), BF16),
        compiler_params=pltpu.CompilerParams(
            dimension_semantics=("arbitrary",), vmem_limit_bytes=_vmem_limit(nbytes)),
        name="norm_modulate",
    )(x, gain.reshape(1, D), mods, mods)


def _rms_gain(xf, gain):
    return (xf * lax.rsqrt(jnp.mean(xf * xf, axis=-1, keepdims=True) + EPS)) * gain


def _softmax_pv(parts, extra_logit=None):
    m_tile = None
    for s, _ in parts:
        for t in range(s.shape[1] // LANES):
            st = s[:, t * LANES:(t + 1) * LANES]
            m_tile = st if m_tile is None else jnp.maximum(m_tile, st)
    m = jnp.max(m_tile, axis=-1, keepdims=True)
    if extra_logit is not None:
        m = jnp.maximum(m, extra_logit)
    acc = None
    for s, v in parts:
        v_ones = jnp.concatenate([v, jnp.ones_like(v)], axis=1)
        ov = jnp.dot(jnp.exp2(s - m).astype(BF16), v_ones, preferred_element_type=F32)
        acc = ov if acc is None else acc + ov
    d = acc.shape[1] // 2
    o, l = acc[:, :d], acc[:, d:]
    if extra_logit is not None:
        l = l + jnp.exp2(extra_logit - m)
    return o / l


def _na_bias_planes(rb_ref, tb_ref, kh):
    W, kw = GRID_W, NA_KW
    shape = (W, 2 * W)
    qc = lax.broadcasted_iota(jnp.int32, shape, 0)
    lane = lax.broadcasted_iota(jnp.int32, shape, 1)
    kc = lane % W
    win = jnp.clip(qc - kw // 2, 0, W - kw)
    in_window = (kc >= win) & (kc < win + kw)
    left = lane < W
    neg = jnp.full(shape, NEG_INF, F32)

    def toeplitz(d, lane0):
        if d < 0 or d > 2 * kh - 2:
            return neg
        row = jnp.broadcast_to(rb_ref[0, d:d + 1, :], shape)
        return pltpu.roll(row, (lane0 - (kw - 1)) % LANES, 1, stride=1, stride_axis=0)

    for p in range(2 * kh + 1):
        both = jnp.where(in_window, jnp.where(left, toeplitz(p - 1, 0), toeplitz(p, W)), NEG_INF)
        tb_ref[0, p] = both
        tb_ref[1, p] = jnp.where(left, both, NEG_INF)
        tb_ref[2, p] = jnp.where(left, NEG_INF, both)


def _na_body(q_ref, k_ref, v_ref, rb_ref, o_ref, tb_ref, sl_ref, sc_ref, *, n_x, n_ctx, rows, kh):
    W = GRID_W
    nkr = kh + 2
    n_steps = rows // 2
    kc = k_ref[n_x:n_x + n_ctx, :]
    vc = v_ref[n_x:n_x + n_ctx, :]
    _na_bias_planes(rb_ref, tb_ref, kh)

    def windows(i):
        r = 2 * i
        a = jnp.clip(r - kh // 2, 0, rows - nkr)
        return (r, a, pl.ds(pl.multiple_of(r * W, 2 * W), 2 * W),
                pl.ds(pl.multiple_of(a * W, 2 * W), nkr * W))

    def scores(i, slot):
        r, a, qrows, krows = windows(i)
        q2 = q_ref[qrows, :]
        s_loc = lax.dot_general(q2, k_ref[krows, :], _NT, preferred_element_type=F32)
        bias_rows = []
        for j in range(2):
            qr = r + j
            r0 = jnp.clip(qr - kh // 2, 0, rows - kh)
            pieces = []
            for t in range(nkr // 2):
                kr0 = a + 2 * t
                in0 = (kr0 >= r0) & (kr0 < r0 + kh)
                in1 = (kr0 + 1 >= r0) & (kr0 + 1 < r0 + kh)
                variant = jnp.where(in0 & in1, 0, jnp.where(in0, 1, 2))
                plane = jnp.where(in0 | in1, jnp.clip(kr0 - qr + kh, 0, 2 * kh - 1), 2 * kh)
                pieces.append(tb_ref[variant, plane])
            bias_rows.append(jnp.concatenate(pieces, axis=1))
        sl_ref[slot] = s_loc + jnp.concatenate(bias_rows, axis=0)
        sc_ref[slot] = lax.dot_general(q2, kc, _NT, preferred_element_type=F32)

    def attend(i, slot):
        _, _, qrows, krows = windows(i)
        o = _softmax_pv([(sl_ref[slot], v_ref[krows, :]), (sc_ref[slot], vc)])
        o_ref[qrows, :] = o.astype(o_ref.dtype)

    scores(0, 0)
    scores(1, 1)

    def pairs(u, carry):
        for ph in range(PAIRS_PER_BODY):
            t = 2 * (PAIRS_PER_BODY * u + ph)
            cur = 2 * (ph % 2)
            for j in range(2):
                scores(jnp.minimum(t + 2 + j, n_steps - 1), 2 - cur + j)
                attend(t + j, cur + j)
        return carry

    lax.fori_loop(0, n_steps // (2 * PAIRS_PER_BODY), pairs, 0)

    ctx_rows = slice(n_x, n_x + n_ctx)
    s = lax.dot_general(q_ref[ctx_rows, :], kc, _NT, preferred_element_type=F32)
    o_ref[ctx_rows, :] = _softmax_pv([(s, vc)]).astype(o_ref.dtype)

def _na_attention(qkv, rel_bias, *, n_x, n_ctx):
    mtot, width = qkv.shape
    H = width // (3 * HEAD_DIM)
    rows = n_x // GRID_W
    kh = min(NA_KH, rows)
    assert GRID_W * 2 == LANES and rows % (4 * PAIRS_PER_BODY) == 0 and rows >= kh + 2
    assert kh == NA_KH and kh % 2 == 0 and PAIRS_PER_BODY % 2 == 0 and SCORE_SLOTS == 4
    assert rel_bias.shape == (H, 2 * kh - 1, 2 * NA_KW - 1) and 2 * NA_KW - 1 <= LANES
    rb = jnp.pad(rel_bias.astype(F32) * LOG2E, ((0, 0), (0, 1), (0, LANES - (2 * NA_KW - 1))))
    blk = (mtot, HEAD_DIM)
    nq, nk = 2 * GRID_W, (kh + 2) * GRID_W
    plane_shape = (3, 2 * kh + 1, GRID_W, 2 * GRID_W)
    nbytes = (8 * mtot * HEAD_DIM * 2 + int(np.prod(plane_shape)) * 4
              + SCORE_SLOTS * nq * (nk + n_ctx) * 4)
    return pl.pallas_call(
        functools.partial(_na_body, n_x=n_x, n_ctx=n_ctx, rows=rows, kh=kh),
        grid=(H,),
        in_specs=[
            pl.BlockSpec(blk, lambda h: (0, h)),
            pl.BlockSpec(blk, lambda h: (0, H + h)),
            pl.BlockSpec(blk, lambda h: (0, 2 * H + h)),
            pl.BlockSpec((1,) + rb.shape[1:], lambda h: (h, 0, 0)),
        ],
        out_specs=pl.BlockSpec(blk, lambda h: (0, h)),
        out_shape=jax.ShapeDtypeStruct((mtot, H * HEAD_DIM), BF16),
        scratch_shapes=[pltpu.VMEM(plane_shape, F32), pltpu.VMEM((SCORE_SLOTS, nq, nk), F32),
                        pltpu.VMEM((SCORE_SLOTS, nq, n_ctx), F32)],
        compiler_params=pltpu.CompilerParams(
            dimension_semantics=("arbitrary",), vmem_limit_bytes=_vmem_limit(nbytes)),
        name="na_attention",
    )(qkv, qkv, qkv, rb)


def _wa_body(q_ref, k_ref, v_ref, ca_ref, sa_ref, cb_ref, sb_ref, band_ref, sink_ref,
             o_ref, kn_ref, sl_ref, sc_ref, *, n_x, n_ctx, group):
    W = WA_WINDOW
    rpb = W // GRID_W
    nblk = n_x // W
    g = pl.program_id(1)
    lane = lax.broadcasted_iota(jnp.int32, (W, HEAD_DIM), 1)
    neg_half = (lane % (HEAD_DIM // 2)) < HEAD_DIM // 4

    def rope(y, b):
        def table(row_ref, col_ref):
            row_part = [jnp.broadcast_to(row_ref[pl.ds(rpb * b + j, 1), :], (GRID_W, HEAD_DIM))
                        for j in range(rpb)]
            return jnp.concatenate(row_part, axis=0) + jnp.concatenate([col_ref[...]] * rpb, axis=0)
        rot = jnp.where(neg_half, -pltpu.roll(y, HEAD_DIM - HEAD_DIM // 4, 1),
                        pltpu.roll(y, HEAD_DIM // 4, 1))
        return y * table(ca_ref, cb_ref) + rot * table(sa_ref, sb_ref)

    @pl.when(g == 0)
    def _prepare_keys():
        def kblock(b, carry):
            r = pl.ds(pl.multiple_of(b * W, W), W)
            kn_ref[r, :] = rope(k_ref[r, :].astype(F32), b).astype(BF16)
            return carry
        lax.fori_loop(0, nblk, kblock, 0, unroll=2)
        kn_ref[n_x:n_x + n_ctx, :] = k_ref[n_x:n_x + n_ctx, :]

    sink = sink_ref[pl.program_id(0) * group + g] * LOG2E
    kc = kn_ref[n_x:n_x + n_ctx, :]
    vc = v_ref[n_x:n_x + n_ctx, :]

    def windows(b):
        a = jnp.clip((b - 1) * W, 0, n_x - 3 * W)
        return pl.ds(pl.multiple_of(b * W, W), W), pl.ds(pl.multiple_of(a, W), 3 * W)

    def scores(b, slot):
        r, krows = windows(b)
        qn = rope(q_ref[r, :].astype(F32), b).astype(BF16)
        s_loc = lax.dot_general(qn, kn_ref[krows, :], _NT, preferred_element_type=F32)
        edge = jnp.where(b == 0, 0, jnp.where(b == nblk - 1, 2, 1))
        sl_ref[slot] = s_loc + band_ref[edge]
        sc_ref[slot] = lax.dot_general(qn, kc, _NT, preferred_element_type=F32)

    def attend(b, slot):
        r, krows = windows(b)
        o = _softmax_pv([(sl_ref[slot], v_ref[krows, :]), (sc_ref[slot], vc)], extra_logit=sink)
        o_ref[r, :] = o.astype(o_ref.dtype)

    scores(0, 0)
    scores(1, 1)

    def pairs(u, carry):
        for ph in range(PAIRS_PER_BODY):
            t = 2 * (PAIRS_PER_BODY * u + ph)
            cur = 2 * (ph % 2)
            for j in range(2):
                scores(jnp.minimum(t + 2 + j, nblk - 1), 2 - cur + j)
                attend(t + j, cur + j)
        return carry

    lax.fori_loop(0, nblk // (2 * PAIRS_PER_BODY), pairs, 0)

    for c in range(n_ctx // W):
        r = slice(n_x + c * W, n_x + (c + 1) * W)
        s = lax.dot_general(q_ref[r, :], kc, _NT, preferred_element_type=F32)
        o_ref[r, :] = _softmax_pv([(s, vc)], extra_logit=sink).astype(o_ref.dtype)


def _rope_tables(rows):
    axis_dim = HEAD_DIM // 2
    inv = ROPE_BASE ** (-jnp.arange(0, axis_dim, 2, dtype=F32) / axis_dim)
    ar = jnp.arange(rows, dtype=F32)[:, None] * inv
    ac = jnp.arange(GRID_W, dtype=F32)[:, None] * inv
    zr = jnp.zeros((rows, axis_dim), F32)
    zc = jnp.zeros((GRID_W, axis_dim), F32)

    def row_tab(f):
        return jnp.concatenate([f(ar), f(ar), zr], axis=-1)

    def col_tab(f):
        return jnp.concatenate([zc, f(ac), f(ac)], axis=-1)

    return row_tab(jnp.cos), row_tab(jnp.sin), col_tab(jnp.cos), col_tab(jnp.sin)


def _wa_band_bias():
    W = WA_WINDOW
    dist = np.arange(3 * W)[None, :] - np.arange(W)[:, None]
    return np.stack([np.where(np.abs(dist + d) <= W, 0.0, NEG_INF) for d in (0, -W, -2 * W)]
                    ).astype(np.float32)


def _wa_attention(qkv, sink, *, n_x, n_ctx, n_heads):
    mtot = qkv.shape[0]
    H, KVH = n_heads, WA_KV_HEADS
    G = H // KVH
    W = WA_WINDOW
    assert W % GRID_W == 0 and n_x % (2 * PAIRS_PER_BODY * W) == 0 and n_ctx % W == 0
    assert W == HEAD_DIM and PAIRS_PER_BODY % 2 == 0 and SCORE_SLOTS == 4
    ca, sa, cb, sb = _rope_tables(n_x // GRID_W)
    band = jnp.asarray(_wa_band_bias())
    blk = (mtot, HEAD_DIM)
    full = lambda arr: pl.BlockSpec(arr.shape, lambda kv, g: (0,) * arr.ndim)
    nbytes = (8 * mtot * HEAD_DIM * 2 + mtot * HEAD_DIM * 2 + 4 * (ca.size + cb.size) * 4
              + 2 * band.size * 4 + SCORE_SLOTS * W * (3 * W + n_ctx) * 4)
    return pl.pallas_call(
        functools.partial(_wa_body, n_x=n_x, n_ctx=n_ctx, group=G),
        grid=(KVH, G),
        in_specs=[
            pl.BlockSpec(blk, lambda kv, g: (0, kv * G + g)),
            pl.BlockSpec(blk, lambda kv, g: (0, H + kv)),
            pl.BlockSpec(blk, lambda kv, g: (0, H + KVH + kv)),
            full(ca), full(sa), full(cb), full(sb), full(band),
            pl.BlockSpec(memory_space=pltpu.SMEM),
        ],
        out_specs=pl.BlockSpec(blk, lambda kv, g: (0, kv * G + g)),
        out_shape=jax.ShapeDtypeStruct((mtot, H * HEAD_DIM), BF16),
        scratch_shapes=[pltpu.VMEM(blk, BF16), pltpu.VMEM((SCORE_SLOTS, W, 3 * W), F32),
                        pltpu.VMEM((SCORE_SLOTS, W, n_ctx), F32)],
        compiler_params=pltpu.CompilerParams(
            dimension_semantics=("arbitrary", "arbitrary"), vmem_limit_bytes=_vmem_limit(nbytes)),
        name="wa_attention",
    )(qkv, qkv, qkv, ca, sa, cb, sb, band, sink.astype(F32))


def _sg_body(u_ref, v_ref, vg_ref, ws_ref, bs_ref, o_ref, *, n_chunks, groups):
    P = SG_CHUNK
    gw = v_ref.shape[1] // groups
    vn = _rms_gain(v_ref[...].astype(F32), vg_ref[...]).astype(BF16)
    for g in range(groups):
        wg = ws_ref[g].astype(BF16)
        cols = slice(g * gw, (g + 1) * gw)
        for c in range(n_chunks):
            rows = slice(c * P, (c + 1) * P)
            sv = jnp.dot(wg, vn[rows, cols], preferred_element_type=F32) + bs_ref[g]
            o_ref[rows, cols] = (u_ref[rows, cols].astype(F32) * sv).astype(o_ref.dtype)


def _spatial_gate(uv, v_gain, w_s, b_s):
    mtot, two_d = uv.shape
    D = two_d // 2
    G, P = SG_GROUPS, SG_CHUNK
    gw = D // G
    assert gw % LANES == 0 and mtot % P == 0
    n_chunks = _largest_divisor(mtot // P, (3, 2, 1))
    bm = n_chunks * P
    bias = jnp.broadcast_to(b_s.astype(F32)[:, :, None], (G, P, gw))
    nbytes = 6 * bm * D * 2 + bm * D * 6 + 2 * (w_s.size + bias.size) * 4
    return pl.pallas_call(
        functools.partial(_sg_body, n_chunks=n_chunks, groups=G),
        grid=(mtot // bm,),
        in_specs=[
            pl.BlockSpec((bm, D), lambda i: (i, 0)),
            pl.BlockSpec((bm, D), lambda i: (i, 1)),
            pl.BlockSpec((1, D), lambda i: (0, 0)),
            pl.BlockSpec(w_s.shape, lambda i: (0, 0, 0)),
            pl.BlockSpec(bias.shape, lambda i: (0, 0, 0)),
        ],
        out_specs=pl.BlockSpec((bm, D), lambda i: (i, 0)),
        out_shape=jax.ShapeDtypeStruct((mtot, D), BF16),
        compiler_params=pltpu.CompilerParams(
            dimension_semantics=("arbitrary",), vmem_limit_bytes=_vmem_limit(nbytes)),
        name="spatial_gate",
    )(uv, uv, v_gain.reshape(1, D), w_s, bias)


def _head_gains(q_gain, k_gain):
    scale = np.float32(HEAD_DIM ** -0.5) * LOG2E
    rows = jnp.stack([q_gain.astype(F32) * scale, k_gain.astype(F32), jnp.ones((HEAD_DIM,), F32)])
    return jnp.concatenate([rows, jnp.zeros((5, HEAD_DIM), F32)], axis=0)


def _ada_modulation(cond, a, b, bias, layer):
    t = _matmul(cond, a, layer, out_dtype=F32, prologue="silu", name="ada_down")
    return _matmul(t, b, layer, out_dtype=F32, epilogue="bias", vec=bias[:, None, :],
                   bn=_largest_divisor(b.shape[2], (2048, 1024, 512, 256, 128)), name="ada_up")


def kernel(x, c, ctx, c_ctx, ada_a, ada_b, ada_bias, norm_mix, norm_mlp, mlp_w1, mlp_w2,
           na_w_qkv, na_w_o, na_q_norm, na_k_norm, na_rel_bias,
           wa_w_qkv, wa_w_o, wa_q_norm, wa_k_norm, wa_sink,
           sg_w_in, sg_v_norm, sg_w_s, sg_b_s, sg_w_out):
    assert x.shape[0] == 1 and ctx.shape[0] == 1 and c.shape[0] == 1
    n_x, D = x.shape[1], x.shape[2]
    n_ctx = ctx.shape[1]
    depth = ada_a.shape[0]
    n_heads = D // HEAD_DIM
    n_mixers = 3

    xc = jnp.concatenate([x[0], ctx[0]], axis=0)
    cond = jnp.zeros((16, D), F32).at[0].set(c[0]).at[1].set(c_ctx)

    for i in range(depth):
        last = i == depth - 1
        kind, j = i % n_mixers, i // n_mixers
        mods = _ada_modulation(cond, ada_a, ada_b, ada_bias, i)
        gate = dict(epilogue="gate_res", vec=mods, n_x=n_x, out_dtype=F32,
                    m_rows=n_x if last else None)

        h = _norm_modulate(xc, norm_mix[i], mods, shift_col=0, scale_col=1, n_x=n_x)
        if kind == 0:
            qkv = _matmul(h, na_w_qkv, j, out_dtype=BF16, epilogue="head_norm",
                          vec=_head_gains(na_q_norm[j], na_k_norm[j]), norm_cols=(D, 2 * D),
                          name="na_qkv")
            o = _na_attention(qkv, na_rel_bias[j], n_x=n_x, n_ctx=n_ctx)
            xc = _matmul(o, na_w_o, j, res=xc, vec_col=2, name="na_out", **gate)
        elif kind == 1:
            qkv = _matmul(h, wa_w_qkv, j, out_dtype=BF16, epilogue="head_norm",
                          vec=_head_gains(wa_q_norm[j], wa_k_norm[j]),
                          norm_cols=(D, D + WA_KV_HEADS * HEAD_DIM), name="wa_qkv")
            o = _wa_attention(qkv, wa_sink[j], n_x=n_x, n_ctx=n_ctx, n_heads=n_heads)
            xc = _matmul(o, wa_w_o, j, res=xc, vec_col=2, name="wa_out", **gate)
        else:
            uv = _matmul(h, sg_w_in, j, out_dtype=BF16, epilogue="gelu", name="sg_in")
            z = _spatial_gate(uv, sg_v_norm[j], sg_w_s[j], sg_b_s[j])
            xc = _matmul(z, sg_w_out, j, res=xc, vec_col=2, name="sg_out", **gate)

        h2 = _norm_modulate(xc, norm_mlp[i], mods, shift_col=3, scale_col=4, n_x=n_x)
        hid = _matmul(h2, mlp_w1, i, out_dtype=BF16, epilogue="relu2", name="mlp_up")
        xc = _matmul(hid, mlp_w2, i, res=xc, vec_col=5, name="mlp_down", **gate)
    return xc[None]
```

```python
import functools

import numpy as np
import jax
import jax.numpy as jnp
from jax import lax
from jax.experimental import pallas as pl
from jax.experimental.pallas import tpu as pltpu

GRID_W = 64
HEAD_DIM = 128
NA_KH = 8
NA_KW = 16
WA_KV_HEADS = 8
WA_WINDOW = 128
ROPE_BASE = 10000.0
SG_GROUPS = 8
SG_CHUNK = 128
EPS = 1e-6
NEG_INF = -1e30
LOG2E = np.float32(np.log2(np.e))
PAIRS_PER_BODY = 8
MIN_CHUNK_ROWS = 128
ROW_CHUNKS = 4
SCORE_SLOTS = 4

LANES = 128
SUBLANES = 8
BF16_ROWS = 16
V7X_VMEM_BYTES = 64 * 1024 * 1024
VMEM_CAP = V7X_VMEM_BYTES - 2 * 1024 * 1024
VMEM_SLACK = 2 * 1024 * 1024

F32 = jnp.float32
BF16 = jnp.bfloat16
_NT = (((1,), (1,)), ((), ()))


def _vmem_limit(nbytes):
    return int(min(VMEM_CAP, nbytes + 4 * VMEM_SLACK))


def _row_chunks(bm):
    if bm % (BF16_ROWS * ROW_CHUNKS) or bm // ROW_CHUNKS < MIN_CHUNK_ROWS:
        return (bm,)
    return (bm // ROW_CHUNKS,) * ROW_CHUNKS


def _largest_divisor(n, candidates):
    for c in candidates:
        if n % c == 0:
            return c
    raise ValueError(f"no tile in {candidates} divides {n}")


def _mm_body(*refs, nk, bm, epilogue, prologue, n_x, norm_tiles, row_chunks):
    x_ref, w_ref = refs[0], refs[1]
    pos = 2
    res_ref = vec_ref = None
    if epilogue == "gate_res":
        res_ref, vec_ref = refs[2], refs[3]
        pos = 4
    elif epilogue in ("bias", "head_norm"):
        vec_ref = refs[2]
        pos = 3
    o_ref = refs[pos]
    acc_ref = refs[pos + 1] if nk > 1 else None
    k = pl.program_id(1)
    m = pl.program_id(2)

    if nk > 1:
        rows = pl.ds(pl.multiple_of(m * bm, bm), bm)

        @pl.when(k == 0)
        def _zero():
            acc_ref[rows, :] = jnp.zeros((bm, acc_ref.shape[1]), F32)

    def matmul_rows(r0, nr):
        xv = x_ref[r0:r0 + nr, :]
        if prologue == "silu":
            xv = xv * jax.nn.sigmoid(xv)
        return jnp.dot(xv.astype(BF16), w_ref[...].astype(BF16), preferred_element_type=F32)

    def finish(acc, r0, nr):
        out_rows = slice(r0, r0 + nr)
        if epilogue == "relu2":
            y = jnp.square(jnp.maximum(acc, 0.0))
        elif epilogue == "gelu":
            y = 0.5 * acc * (1.0 + lax.erf(acc * np.float32(np.sqrt(0.5))))
        elif epilogue == "gate_res":
            row_id = m * bm + r0 + lax.broadcasted_iota(jnp.int32, (nr, 1), 0)
            gate = jnp.where(row_id >= n_x, vec_ref[1:2, :], vec_ref[0:1, :])
            y = res_ref[out_rows, :] + gate * acc
        elif epilogue == "bias":
            y = acc + vec_ref[...]
        elif epilogue == "head_norm":
            n = pl.program_id(0)
            kind = (n >= norm_tiles[0]).astype(jnp.int32) + (n >= norm_tiles[1]).astype(jnp.int32)
            gain = vec_ref[pl.ds(kind, 1), :]
            for t in range(acc.shape[1] // HEAD_DIM):
                cols = slice(t * HEAD_DIM, (t + 1) * HEAD_DIM)
                a = acc[:, cols]
                inv_rms = lax.rsqrt(jnp.mean(a * a, axis=-1, keepdims=True) + EPS)
                inv_rms = jnp.where(kind < 2, inv_rms, 1.0)
                o_ref[out_rows, cols] = ((a * inv_rms) * gain).astype(o_ref.dtype)
            return
        else:
            y = acc
        o_ref[out_rows, :] = y.astype(o_ref.dtype)

    if nk == 1:
        r0 = 0
        for nr in row_chunks:
            finish(matmul_rows(r0, nr), r0, nr)
            r0 += nr
    else:
        acc_ref[rows, :] = acc_ref[rows, :] + matmul_rows(0, bm)

        @pl.when(k == nk - 1)
        def _emit():
            finish(acc_ref[rows, :], 0, bm)


def _matmul(x, w, layer, *, out_dtype, epilogue=None, prologue=None, res=None, vec=None, vec_col=0,
            n_x=0, m_rows=None, norm_cols=None, bm=None, bn=None, bk=None, name="mm"):
    M = x.shape[0] if m_rows is None else m_rows
    _, K, N = w.shape
    bk = bk or _largest_divisor(K, (4096, 2048, 1024, 512, 256, 128))
    nk = K // bk
    last_k = nk - 1
    out_bytes = jnp.dtype(out_dtype).itemsize

    def vmem_bytes(bm_, bn_):
        n = 2 * bm_ * bk * x.dtype.itemsize + 2 * bk * bn_ * 4 + 2 * bm_ * bn_ * out_bytes
        if epilogue == "gate_res":
            n += 2 * bm_ * bn_ * 4 + 2 * vec.shape[0] * bn_ * 4
        if nk > 1:
            n += M * bn_ * 4
        return n

    def best_bm(bn_):
        return next(c for c in range(M, 0, -BF16_ROWS)
                    if M % c == 0 and vmem_bytes(c, bn_) <= VMEM_CAP - VMEM_SLACK)

    if bn is None:
        col_unit = int(np.gcd.reduce((N,) + tuple(norm_cols or ())))
        bn = _largest_divisor(col_unit, (512, 256, 128))
        if (nk == 1 and bm is None and epilogue == "gate_res" and col_unit % (2 * bn) == 0
                and 2 * best_bm(2 * bn) > best_bm(bn)):
            bn *= 2
    if bm is None:
        bm = best_bm(bn)

    def out_rows(k, m):
        return m if nk == 1 else jnp.where(k == last_k, m, 0)

    in_specs = [
        pl.BlockSpec((bm, bk), lambda n, k, m: (m, k)),
        pl.BlockSpec((None, bk, bn), lambda n, k, m: (layer, k, n)),
    ]
    args = [x, w]
    if epilogue == "gate_res":
        in_specs.append(pl.BlockSpec((bm, bn), lambda n, k, m: (out_rows(k, m), n)))
        in_specs.append(pl.BlockSpec((vec.shape[0], bn), lambda n, k, m: (0, vec_col * (N // bn) + n)))
        args += [res, vec]
    elif epilogue == "bias":
        in_specs.append(pl.BlockSpec((None, 1, bn), lambda n, k, m: (layer, 0, n)))
        args.append(vec)
    norm_tiles = None
    if epilogue == "head_norm":
        in_specs.append(pl.BlockSpec(vec.shape, lambda n, k, m: (0, 0)))
        args.append(vec)
        assert norm_cols[0] % bn == 0 and norm_cols[1] % bn == 0 and bn % HEAD_DIM == 0
        norm_tiles = (norm_cols[0] // bn, norm_cols[1] // bn)
    scratch = [pltpu.VMEM((M, bn), F32)] if nk > 1 else []
    row_chunks = _row_chunks(bm)
    body = functools.partial(_mm_body, nk=nk, bm=bm, epilogue=epilogue, prologue=prologue, n_x=n_x,
                             norm_tiles=norm_tiles, row_chunks=row_chunks)
    return pl.pallas_call(
        body,
        grid=(N // bn, nk, M // bm),
        in_specs=in_specs,
        out_specs=pl.BlockSpec((bm, bn), lambda n, k, m: (out_rows(k, m), n)),
        out_shape=jax.ShapeDtypeStruct((M, N), out_dtype),
        scratch_shapes=scratch,
        compiler_params=pltpu.CompilerParams(
            dimension_semantics=("arbitrary", "arbitrary", "arbitrary"),
            vmem_limit_bytes=_vmem_limit(vmem_bytes(bm, bn))),
        name=name,
    )(*args)


def _norm_body(x_ref, g_ref, sh_ref, sc_ref, o_ref, *, n_x):
    bm = x_ref.shape[0]
    chunk = SUBLANES
    assert n_x % chunk == 0

    def rows_chunk(c, carry):
        r = pl.ds(pl.multiple_of(c * chunk, chunk), chunk)
        t = (pl.program_id(0) * bm + c * chunk >= n_x).astype(jnp.int32)
        x = x_ref[r, :]
        y = x * lax.rsqrt(jnp.mean(x * x, axis=-1, keepdims=True) + EPS)
        y = y * g_ref[...]
        o_ref[r, :] = (y * (1.0 + sc_ref[pl.ds(t, 1), :]) + sh_ref[pl.ds(t, 1), :]).astype(o_ref.dtype)
        return carry

    lax.fori_loop(0, bm // chunk, rows_chunk, 0, unroll=4)


def _norm_modulate(x, gain, mods, *, shift_col, scale_col, n_x):
    M, D = x.shape
    bm = _largest_divisor(M, (1024, 768, 512, 384, 256, 128, 64, 32, 16))
    nbytes = 2 * bm * D * 4 + 2 * bm * D * 2 + 6 * 8 * D * 4
    return pl.pallas_call(
        functools.partial(_norm_body, n_x=n_x),
        grid=(M // bm,),
        in_specs=[
            pl.BlockSpec((bm, D), lambda i: (i, 0)),
            pl.BlockSpec((1, D), lambda i: (0, 0)),
            pl.BlockSpec((mods.shape[0], D), lambda i: (0, shift_col)),
            pl.BlockSpec((mods.shape[0], D), lambda i: (0, scale_col)),
        ],
        out_specs=pl.BlockSpec((bm, D), lambda i: (i, 0)),
        out_shape=jax.ShapeDtypeStruct((M, D), BF16),
        compiler_params=pltpu.CompilerParams(
            dimension_semantics=("arbitrary",), vmem_limit_bytes=_vmem_limit(nbytes)),
        name="norm_modulate",
    )(x, gain.reshape(1, D), mods, mods)


def _rms_gain(xf, gain):
    return (xf * lax.rsqrt(jnp.mean(xf * xf, axis=-1, keepdims=True) + EPS)) * gain


def _softmax_pv(parts, extra_logit=None):
    m_tile = None
    for s, _ in parts:
        for t in range(s.shape[1] // LANES):
            st = s[:, t * LANES:(t + 1) * LANES]
            m_tile = st if m_tile is None else jnp.maximum(m_tile, st)
    m = jnp.max(m_tile, axis=-1, keepdims=True)
    if extra_logit is not None:
        m = jnp.maximum(m, extra_logit)
    acc = None
    for s, v in parts:
        v_ones = jnp.concatenate([v, jnp.ones_like(v)], axis=1)
        ov = jnp.dot(jnp.exp2(s - m).astype(BF16), v_ones, preferred_element_type=F32)
        acc = ov if acc is None else acc + ov
    d = acc.shape[1] // 2
    o, l = acc[:, :d], acc[:, d:]
    if extra_logit is not None:
        l = l + jnp.exp2(extra_logit - m)
    return o / l


def _na_bias_planes(rb_ref, tb_ref, kh):
    W, kw = GRID_W, NA_KW
    shape = (W, 2 * W)
    qc = lax.broadcasted_iota(jnp.int32, shape, 0)
    lane = lax.broadcasted_iota(jnp.int32, shape, 1)
    kc = lane % W
    win = jnp.clip(qc - kw // 2, 0, W - kw)
    in_window = (kc >= win) & (kc < win + kw)
    left = lane < W
    neg = jnp.full(shape, NEG_INF, F32)

    def toeplitz(d, lane0):
        if d < 0 or d > 2 * kh - 2:
            return neg
        row = jnp.broadcast_to(rb_ref[0, d:d + 1, :], shape)
        return pltpu.roll(row, (lane0 - (kw - 1)) % LANES, 1, stride=1, stride_axis=0)

    for p in range(2 * kh + 1):
        both = jnp.where(in_window, jnp.where(left, toeplitz(p - 1, 0), toeplitz(p, W)), NEG_INF)
        tb_ref[0, p] = both
        tb_ref[1, p] = jnp.where(left, both, NEG_INF)
        tb_ref[2, p] = jnp.where(left, NEG_INF, both)


def _na_body(q_ref, k_ref, v_ref, rb_ref, o_ref, tb_ref, sl_ref, sc_ref, *, n_x, n_ctx, rows, kh):
    W = GRID_W
    nkr = kh + 2
    n_steps = rows // 2
    kc = k_ref[n_x:n_x + n_ctx, :]
    vc = v_ref[n_x:n_x + n_ctx, :]
    _na_bias_planes(rb_ref, tb_ref, kh)

    def windows(i):
        r = 2 * i
        a = jnp.clip(r - kh // 2, 0, rows - nkr)
        return (r, a, pl.ds(pl.multiple_of(r * W, 2 * W), 2 * W),
                pl.ds(pl.multiple_of(a * W, 2 * W), nkr * W))

    def scores(i, slot):
        r, a, qrows, krows = windows(i)
        q2 = q_ref[qrows, :]
        s_loc = lax.dot_general(q2, k_ref[krows, :], _NT, preferred_element_type=F32)
        bias_rows = []
        for j in range(2):
            qr = r + j
            r0 = jnp.clip(qr - kh // 2, 0, rows - kh)
            pieces = []
            for t in range(nkr // 2):
                kr0 = a + 2 * t
                in0 = (kr0 >= r0) & (kr0 < r0 + kh)
                in1 = (kr0 + 1 >= r0) & (kr0 + 1 < r0 + kh)
                variant = jnp.where(in0 & in1, 0, jnp.where(in0, 1, 2))
                plane = jnp.where(in0 | in1, jnp.clip(kr0 - qr + kh, 0, 2 * kh - 1), 2 * kh)
                pieces.append(tb_ref[variant, plane])
            bias_rows.append(jnp.concatenate(pieces, axis=1))
        sl_ref[slot] = s_loc + jnp.concatenate(bias_rows, axis=0)
        sc_ref[slot] = lax.dot_general(q2, kc, _NT, preferred_element_type=F32)

    def attend(i, slot):
        _, _, qrows, krows = windows(i)
        o = _softmax_pv([(sl_ref[slot], v_ref[krows, :]), (sc_ref[slot], vc)])
        o_ref[qrows, :] = o.astype(o_ref.dtype)

    scores(0, 0)
    scores(1, 1)

    def pairs(u, carry):
        for ph in range(PAIRS_PER_BODY):
            t = 2 * (PAIRS_PER_BODY * u + ph)
            cur = 2 * (ph % 2)
            for j in range(2):
                scores(jnp.minimum(t + 2 + j, n_steps - 1), 2 - cur + j)
                attend(t + j, cur + j)
        return carry

    lax.fori_loop(0, n_steps // (2 * PAIRS_PER_BODY), pairs, 0)

    ctx_rows = slice(n_x, n_x + n_ctx)
    s = lax.dot_general(q_ref[ctx_rows, :], kc, _NT, preferred_element_type=F32)
    o_ref[ctx_rows, :] = _softmax_pv([(s, vc)]).astype(o_ref.dtype)

def _na_attention(qkv, rel_bias, *, n_x, n_ctx):
    mtot, width = qkv.shape
    H = width // (3 * HEAD_DIM)
    rows = n_x // GRID_W
    kh = min(NA_KH, rows)
    assert GRID_W * 2 == LANES and rows % (4 * PAIRS_PER_BODY) == 0 and rows >= kh + 2
    assert kh == NA_KH and kh % 2 == 0 and PAIRS_PER_BODY % 2 == 0 and SCORE_SLOTS == 4
    assert rel_bias.shape == (H, 2 * kh - 1, 2 * NA_KW - 1) and 2 * NA_KW - 1 <= LANES
    rb = jnp.pad(rel_bias.astype(F32) * LOG2E, ((0, 0), (0, 1), (0, LANES - (2 * NA_KW - 1))))
    blk = (mtot, HEAD_DIM)
    nq, nk = 2 * GRID_W, (kh + 2) * GRID_W
    plane_shape = (3, 2 * kh + 1, GRID_W, 2 * GRID_W)
    nbytes = (8 * mtot * HEAD_DIM * 2 + int(np.prod(plane_shape)) * 4
              + SCORE_SLOTS * nq * (nk + n_ctx) * 4)
    return pl.pallas_call(
        functools.partial(_na_body, n_x=n_x, n_ctx=n_ctx, rows=rows, kh=kh),
        grid=(H,),
        in_specs=[
            pl.BlockSpec(blk, lambda h: (0, h)),
            pl.BlockSpec(blk, lambda h: (0, H + h)),
            pl.BlockSpec(blk, lambda h: (0, 2 * H + h)),
            pl.BlockSpec((1,) + rb.shape[1:], lambda h: (h, 0, 0)),
        ],
        out_specs=pl.BlockSpec(blk, lambda h: (0, h)),
        out_shape=jax.ShapeDtypeStruct((mtot, H * HEAD_DIM), BF16),
        scratch_shapes=[pltpu.VMEM(plane_shape, F32), pltpu.VMEM((SCORE_SLOTS, nq, nk), F32),
                        pltpu.VMEM((SCORE_SLOTS, nq, n_ctx), F32)],
        compiler_params=pltpu.CompilerParams(
            dimension_semantics=("arbitrary",), vmem_limit_bytes=_vmem_limit(nbytes)),
        name="na_attention",
    )(qkv, qkv, qkv, rb)


def _wa_body(q_ref, k_ref, v_ref, ca_ref, sa_ref, cb_ref, sb_ref, band_ref, sink_ref,
             o_ref, kn_ref, sl_ref, sc_ref, *, n_x, n_ctx, group):
    W = WA_WINDOW
    rpb = W // GRID_W
    nblk = n_x // W
    g = pl.program_id(1)
    lane = lax.broadcasted_iota(jnp.int32, (W, HEAD_DIM), 1)
    neg_half = (lane % (HEAD_DIM // 2)) < HEAD_DIM // 4

    def rope(y, b):
        def table(row_ref, col_ref):
            row_part = [jnp.broadcast_to(row_ref[pl.ds(rpb * b + j, 1), :], (GRID_W, HEAD_DIM))
                        for j in range(rpb)]
            return jnp.concatenate(row_part, axis=0) + jnp.concatenate([col_ref[...]] * rpb, axis=0)
        rot = jnp.where(neg_half, -pltpu.roll(y, HEAD_DIM - HEAD_DIM // 4, 1),
                        pltpu.roll(y, HEAD_DIM // 4, 1))
        return y * table(ca_ref, cb_ref) + rot * table(sa_ref, sb_ref)

    @pl.when(g == 0)
    def _prepare_keys():
        def kblock(b, carry):
            r = pl.ds(pl.multiple_of(b * W, W), W)
            kn_ref[r, :] = rope(k_ref[r, :].astype(F32), b).astype(BF16)
            return carry
        lax.fori_loop(0, nblk, kblock, 0, unroll=2)
        kn_ref[n_x:n_x + n_ctx, :] = k_ref[n_x:n_x + n_ctx, :]

    sink = sink_ref[pl.program_id(0) * group + g] * LOG2E
    kc = kn_ref[n_x:n_x + n_ctx, :]
    vc = v_ref[n_x:n_x + n_ctx, :]

    def windows(b):
        a = jnp.clip((b - 1) * W, 0, n_x - 3 * W)
        return pl.ds(pl.multiple_of(b * W, W), W), pl.ds(pl.multiple_of(a, W), 3 * W)

    def scores(b, slot):
        r, krows = windows(b)
        qn = rope(q_ref[r, :].astype(F32), b).astype(BF16)
        s_loc = lax.dot_general(qn, kn_ref[krows, :], _NT, preferred_element_type=F32)
        edge = jnp.where(b == 0, 0, jnp.where(b == nblk - 1, 2, 1))
        sl_ref[slot] = s_loc + band_ref[edge]
        sc_ref[slot] = lax.dot_general(qn, kc, _NT, preferred_element_type=F32)

    def attend(b, slot):
        r, krows = windows(b)
        o = _softmax_pv([(sl_ref[slot], v_ref[krows, :]), (sc_ref[slot], vc)], extra_logit=sink)
        o_ref[r, :] = o.astype(o_ref.dtype)

    scores(0, 0)
    scores(1, 1)

    def pairs(u, carry):
        for ph in range(PAIRS_PER_BODY):
            t = 2 * (PAIRS_PER_BODY * u + ph)
            cur = 2 * (ph % 2)
            for j in range(2):
                scores(jnp.minimum(t + 2 + j, nblk - 1), 2 - cur + j)
                attend(t + j, cur + j)
        return carry

    lax.fori_loop(0, nblk // (2 * PAIRS_PER_BODY), pairs, 0)

    for c in range(n_ctx // W):
        r = slice(n_x + c * W, n_x + (c + 1) * W)
        s = lax.dot_general(q_ref[r, :], kc, _NT, preferred_element_type=F32)
        o_ref[r, :] = _softmax_pv([(s, vc)], extra_logit=sink).astype(o_ref.dtype)


def _rope_tables(rows):
    axis_dim = HEAD_DIM // 2
    inv = ROPE_BASE ** (-jnp.arange(0, axis_dim, 2, dtype=F32) / axis_dim)
    ar = jnp.arange(rows, dtype=F32)[:, None] * inv
    ac = jnp.arange(GRID_W, dtype=F32)[:, None] * inv
    zr = jnp.zeros((rows, axis_dim), F32)
    zc = jnp.zeros((GRID_W, axis_dim), F32)

    def row_tab(f):
        return jnp.concatenate([f(ar), f(ar), zr], axis=-1)

    def col_tab(f):
        return jnp.concatenate([zc, f(ac), f(ac)], axis=-1)

    return row_tab(jnp.cos), row_tab(jnp.sin), col_tab(jnp.cos), col_tab(jnp.sin)


def _wa_band_bias():
    W = WA_WINDOW
    dist = np.arange(3 * W)[None, :] - np.arange(W)[:, None]
    return np.stack([np.where(np.abs(dist + d) <= W, 0.0, NEG_INF) for d in (0, -W, -2 * W)]
                    ).astype(np.float32)


def _wa_attention(qkv, sink, *, n_x, n_ctx, n_heads):
    mtot = qkv.shape[0]
    H, KVH = n_heads, WA_KV_HEADS
    G = H // KVH
    W = WA_WINDOW
    assert W % GRID_W == 0 and n_x % (2 * PAIRS_PER_BODY * W) == 0 and n_ctx % W == 0
    assert W == HEAD_DIM and PAIRS_PER_BODY % 2 == 0 and SCORE_SLOTS == 4
    ca, sa, cb, sb = _rope_tables(n_x // GRID_W)
    band = jnp.asarray(_wa_band_bias())
    blk = (mtot, HEAD_DIM)
    full = lambda arr: pl.BlockSpec(arr.shape, lambda kv, g: (0,) * arr.ndim)
    nbytes = (8 * mtot * HEAD_DIM * 2 + mtot * HEAD_DIM * 2 + 4 * (ca.size + cb.size) * 4
              + 2 * band.size * 4 + SCORE_SLOTS * W * (3 * W + n_ctx) * 4)
    return pl.pallas_call(
        functools.partial(_wa_body, n_x=n_x, n_ctx=n_ctx, group=G),
        grid=(KVH, G),
        in_specs=[
            pl.BlockSpec(blk, lambda kv, g: (0, kv * G + g)),
            pl.BlockSpec(blk, lambda kv, g: (0, H + kv)),
            pl.BlockSpec(blk, lambda kv, g: (0, H + KVH + kv)),
            full(ca), full(sa), full(cb), full(sb), full(band),
            pl.BlockSpec(memory_space=pltpu.SMEM),
        ],
        out_specs=pl.BlockSpec(blk, lambda kv, g: (0, kv * G + g)),
        out_shape=jax.ShapeDtypeStruct((mtot, H * HEAD_DIM), BF16),
        scratch_shapes=[pltpu.VMEM(blk, BF16), pltpu.VMEM((SCORE_SLOTS, W, 3 * W), F32),
                        pltpu.VMEM((SCORE_SLOTS, W, n_ctx), F32)],
        compiler_params=pltpu.CompilerParams(
            dimension_semantics=("arbitrary", "arbitrary"), vmem_limit_bytes=_vmem_limit(nbytes)),
        name="wa_attention",
    )(qkv, qkv, qkv, ca, sa, cb, sb, band, sink.astype(F32))


def _sg_body(u_ref, v_ref, vg_ref, ws_ref, bs_ref, o_ref, *, n_chunks, groups):
    P = SG_CHUNK
    gw = v_ref.shape[1] // groups
    vn = _rms_gain(v_ref[...].astype(F32), vg_ref[...]).astype(BF16)
    for g in range(groups):
        wg = ws_ref[g].astype(BF16)
        cols = slice(g * gw, (g + 1) * gw)
        for c in range(n_chunks):
            rows = slice(c * P, (c + 1) * P)
            sv = jnp.dot(wg, vn[rows, cols], preferred_element_type=F32) + bs_ref[g]
            o_ref[rows, cols] = (u_ref[rows, cols].astype(F32) * sv).astype(o_ref.dtype)


def _spatial_gate(uv, v_gain, w_s, b_s):
    mtot, two_d = uv.shape
    D = two_d // 2
    G, P = SG_GROUPS, SG_CHUNK
    gw = D // G
    assert gw % LANES == 0 and mtot % P == 0
    n_chunks = _largest_divisor(mtot // P, (3, 2, 1))
    bm = n_chunks * P
    bias = jnp.broadcast_to(b_s.astype(F32)[:, :, None], (G, P, gw))
    nbytes = 6 * bm * D * 2 + bm * D * 6 + 2 * (w_s.size + bias.size) * 4
    return pl.pallas_call(
        functools.partial(_sg_body, n_chunks=n_chunks, groups=G),
        grid=(mtot // bm,),
        in_specs=[
            pl.BlockSpec((bm, D), lambda i: (i, 0)),
            pl.BlockSpec((bm, D), lambda i: (i, 1)),
            pl.BlockSpec((1, D), lambda i: (0, 0)),
            pl.BlockSpec(w_s.shape, lambda i: (0, 0, 0)),
            pl.BlockSpec(bias.shape, lambda i: (0, 0, 0)),
        ],
        out_specs=pl.BlockSpec((bm, D), lambda i: (i, 0)),
        out_shape=jax.ShapeDtypeStruct((mtot, D), BF16),
        compiler_params=pltpu.CompilerParams(
            dimension_semantics=("arbitrary",), vmem_limit_bytes=_vmem_limit(nbytes)),
        name="spatial_gate",
    )(uv, uv, v_gain.reshape(1, D), w_s, bias)


def _head_gains(q_gain, k_gain):
    scale = np.float32(HEAD_DIM ** -0.5) * LOG2E
    rows = jnp.stack([q_gain.astype(F32) * scale, k_gain.astype(F32), jnp.ones((HEAD_DIM,), F32)])
    return jnp.concatenate([rows, jnp.zeros((5, HEAD_DIM), F32)], axis=0)


def _ada_modulation(cond, a, b, bias, layer):
    t = _matmul(cond, a, layer, out_dtype=F32, prologue="silu", name="ada_down")
    return _matmul(t, b, layer, out_dtype=F32, epilogue="bias", vec=bias[:, None, :],
                   bn=_largest_divisor(b.shape[2], (2048, 1024, 512, 256, 128)), name="ada_up")


def kernel(x, c, ctx, c_ctx, ada_a, ada_b, ada_bias, norm_mix, norm_mlp, mlp_w1, mlp_w2,
           na_w_qkv, na_w_o, na_q_norm, na_k_norm, na_rel_bias,
           wa_w_qkv, wa_w_o, wa_q_norm, wa_k_norm, wa_sink,
           sg_w_in, sg_v_norm, sg_w_s, sg_b_s, sg_w_out):
    assert x.shape[0] == 1 and ctx.shape[0] == 1 and c.shape[0] == 1
    n_x, D = x.shape[1], x.shape[2]
    n_ctx = ctx.shape[1]
    depth = ada_a.shape[0]
    n_heads = D // HEAD_DIM
    n_mixers = 3

    xc = jnp.concatenate([x[0], ctx[0]], axis=0)
    cond = jnp.zeros((16, D), F32).at[0].set(c[0]).at[1].set(c_ctx)

    for i in range(depth):
        last = i == depth - 1
        kind, j = i % n_mixers, i // n_mixers
        mods = _ada_modulation(cond, ada_a, ada_b, ada_bias, i)
        gate = dict(epilogue="gate_res", vec=mods, n_x=n_x, out_dtype=F32,
                    m_rows=n_x if last else None)

        h = _norm_modulate(xc, norm_mix[i], mods, shift_col=0, scale_col=1, n_x=n_x)
        if kind == 0:
            qkv = _matmul(h, na_w_qkv, j, out_dtype=BF16, epilogue="head_norm",
                          vec=_head_gains(na_q_norm[j], na_k_norm[j]), norm_cols=(D, 2 * D),
                          name="na_qkv")
            o = _na_attention(qkv, na_rel_bias[j], n_x=n_x, n_ctx=n_ctx)
            xc = _matmul(o, na_w_o, j, res=xc, vec_col=2, name="na_out", **gate)
        elif kind == 1:
            qkv = _matmul(h, wa_w_qkv, j, out_dtype=BF16, epilogue="head_norm",
                          vec=_head_gains(wa_q_norm[j], wa_k_norm[j]),
                          norm_cols=(D, D + WA_KV_HEADS * HEAD_DIM), name="wa_qkv")
            o = _wa_attention(qkv, wa_sink[j], n_x=n_x, n_ctx=n_ctx, n_heads=n_heads)
            xc = _matmul(o, wa_w_o, j, res=xc, vec_col=2, name="wa_out", **gate)
        else:
            uv = _matmul(h, sg_w_in, j, out_dtype=BF16, epilogue="gelu", name="sg_in")
            z = _spatial_gate(uv, sg_v_norm[j], sg_w_s[j], sg_b_s[j])
            xc = _matmul(z, sg_w_out, j, res=xc, vec_col=2, name="sg_out", **gate)

        h2 = _norm_modulate(xc, norm_mlp[i], mods, shift_col=3, scale_col=4, n_x=n_x)
        hid = _matmul(h2, mlp_w1, i, out_dtype=BF16, epilogue="relu2", name="mlp_up")
        xc = _matmul(hid, mlp_w2, i, res=xc, vec_col=5, name="mlp_down", **gate)
    return xc[None]
```

```python
import functools

import numpy as np
import jax
import jax.numpy as jnp
from jax import lax
from jax.experimental import pallas as pl
from jax.experimental.pallas import tpu as pltpu

GRID_W = 64
HEAD_DIM = 128
NA_KH = 8
NA_KW = 16
WA_KV_HEADS = 8
WA_WINDOW = 128
ROPE_BASE = 10000.0
SG_GROUPS = 8
SG_CHUNK = 128
EPS = 1e-6
NEG_INF = -1e30
LOG2E = np.float32(np.log2(np.e))
PAIRS_PER_BODY = 16
MIN_CHUNK_ROWS = 128
ROW_CHUNKS = 4
SCORE_SLOTS = 4

LANES = 128
SUBLANES = 8
BF16_ROWS = 16
V7X_VMEM_BYTES = 64 * 1024 * 1024
VMEM_CAP = V7X_VMEM_BYTES - 2 * 1024 * 1024
VMEM_SLACK = 2 * 1024 * 1024

F32 = jnp.float32
BF16 = jnp.bfloat16
_NT = (((1,), (1,)), ((), ()))


def _vmem_limit(nbytes):
    return int(min(VMEM_CAP, nbytes + 4 * VMEM_SLACK))


def _row_chunks(bm):
    if bm % (BF16_ROWS * ROW_CHUNKS) or bm // ROW_CHUNKS < MIN_CHUNK_ROWS:
        return (bm,)
    return (bm // ROW_CHUNKS,) * ROW_CHUNKS


def _largest_divisor(n, candidates):
    for c in candidates:
        if n % c == 0:
            return c
    raise ValueError(f"no tile in {candidates} divides {n}")


def _mm_body(*refs, nk, bm, epilogue, prologue, n_x, norm_tiles, row_chunks):
    x_ref, w_ref = refs[0], refs[1]
    pos = 2
    res_ref = vec_ref = None
    if epilogue == "gate_res":
        res_ref, vec_ref = refs[2], refs[3]
        pos = 4
    elif epilogue in ("bias", "head_norm"):
        vec_ref = refs[2]
        pos = 3
    o_ref = refs[pos]
    acc_ref = refs[pos + 1] if nk > 1 else None
    k = pl.program_id(1)
    m = pl.program_id(2)

    if nk > 1:
        rows = pl.ds(pl.multiple_of(m * bm, bm), bm)

        @pl.when(k == 0)
        def _zero():
            acc_ref[rows, :] = jnp.zeros((bm, acc_ref.shape[1]), F32)

    def matmul_rows(r0, nr):
        xv = x_ref[r0:r0 + nr, :]
        if prologue == "silu":
            xv = xv * jax.nn.sigmoid(xv)
        return jnp.dot(xv.astype(BF16), w_ref[...].astype(BF16), preferred_element_type=F32)

    def finish(acc, r0, nr):
        out_rows = slice(r0, r0 + nr)
        if epilogue == "relu2":
            y = jnp.square(jnp.maximum(acc, 0.0))
        elif epilogue == "gelu":
            y = 0.5 * acc * (1.0 + lax.erf(acc * np.float32(np.sqrt(0.5))))
        elif epilogue == "gate_res":
            row_id = m * bm + r0 + lax.broadcasted_iota(jnp.int32, (nr, 1), 0)
            gate = jnp.where(row_id >= n_x, vec_ref[1:2, :], vec_ref[0:1, :])
            y = res_ref[out_rows, :] + gate * acc
        elif epilogue == "bias":
            y = acc + vec_ref[...]
        elif epilogue == "head_norm":
            n = pl.program_id(0)
            kind = (n >= norm_tiles[0]).astype(jnp.int32) + (n >= norm_tiles[1]).astype(jnp.int32)
            gain = vec_ref[pl.ds(kind, 1), :]
            for t in range(acc.shape[1] // HEAD_DIM):
                cols = slice(t * HEAD_DIM, (t + 1) * HEAD_DIM)
                a = acc[:, cols]
                inv_rms = lax.rsqrt(jnp.mean(a * a, axis=-1, keepdims=True) + EPS)
                inv_rms = jnp.where(kind < 2, inv_rms, 1.0)
                o_ref[out_rows, cols] = ((a * inv_rms) * gain).astype(o_ref.dtype)
            return
        else:
            y = acc
        o_ref[out_rows, :] = y.astype(o_ref.dtype)

    if nk == 1:
        r0 = 0
        for nr in row_chunks:
            finish(matmul_rows(r0, nr), r0, nr)
            r0 += nr
    else:
        acc_ref[rows, :] = acc_ref[rows, :] + matmul_rows(0, bm)

        @pl.when(k == nk - 1)
        def _emit():
            finish(acc_ref[rows, :], 0, bm)


def _matmul(x, w, layer, *, out_dtype, epilogue=None, prologue=None, res=None, vec=None, vec_col=0,
            n_x=0, m_rows=None, norm_cols=None, bm=None, bn=None, bk=None, name="mm"):
    M = x.shape[0] if m_rows is None else m_rows
    _, K, N = w.shape
    bk = bk or _largest_divisor(K, (4096, 2048, 1024, 512, 256, 128))
    nk = K // bk
    last_k = nk - 1
    out_bytes = jnp.dtype(out_dtype).itemsize

    def vmem_bytes(bm_, bn_):
        n = 2 * bm_ * bk * x.dtype.itemsize + 2 * bk * bn_ * 4 + 2 * bm_ * bn_ * out_bytes
        if epilogue == "gate_res":
            n += 2 * bm_ * bn_ * 4 + 2 * vec.shape[0] * bn_ * 4
        if nk > 1:
            n += M * bn_ * 4
        return n

    def best_bm(bn_):
        return next(c for c in range(M, 0, -BF16_ROWS)
                    if M % c == 0 and vmem_bytes(c, bn_) <= VMEM_CAP - VMEM_SLACK)

    if bn is None:
        col_unit = int(np.gcd.reduce((N,) + tuple(norm_cols or ())))
        bn = _largest_divisor(col_unit, (512, 256, 128))
        if (nk == 1 and bm is None and epilogue == "gate_res" and col_unit % (2 * bn) == 0
                and 2 * best_bm(2 * bn) > best_bm(bn)):
            bn *= 2
    if bm is None:
        bm = best_bm(bn)

    def out_rows(k, m):
        return m if nk == 1 else jnp.where(k == last_k, m, 0)

    in_specs = [
        pl.BlockSpec((bm, bk), lambda n, k, m: (m, k)),
        pl.BlockSpec((None, bk, bn), lambda n, k, m: (layer, k, n)),
    ]
    args = [x, w]
    if epilogue == "gate_res":
        in_specs.append(pl.BlockSpec((bm, bn), lambda n, k, m: (out_rows(k, m), n)))
        in_specs.append(pl.BlockSpec((vec.shape[0], bn), lambda n, k, m: (0, vec_col * (N // bn) + n)))
        args += [res, vec]
    elif epilogue == "bias":
        in_specs.append(pl.BlockSpec((None, 1, bn), lambda n, k, m: (layer, 0, n)))
        args.append(vec)
    norm_tiles = None
    if epilogue == "head_norm":
        in_specs.append(pl.BlockSpec(vec.shape, lambda n, k, m: (0, 0)))
        args.append(vec)
        assert norm_cols[0] % bn == 0 and norm_cols[1] % bn == 0 and bn % HEAD_DIM == 0
        norm_tiles = (norm_cols[0] // bn, norm_cols[1] // bn)
    scratch = [pltpu.VMEM((M, bn), F32)] if nk > 1 else []
    row_chunks = _row_chunks(bm)
    body = functools.partial(_mm_body, nk=nk, bm=bm, epilogue=epilogue, prologue=prologue, n_x=n_x,
                             norm_tiles=norm_tiles, row_chunks=row_chunks)
    return pl.pallas_call(
        body,
        grid=(N // bn, nk, M // bm),
        in_specs=in_specs,
        out_specs=pl.BlockSpec((bm, bn), lambda n, k, m: (out_rows(k, m), n)),
        out_shape=jax.ShapeDtypeStruct((M, N), out_dtype),
        scratch_shapes=scratch,
        compiler_params=pltpu.CompilerParams(
            dimension_semantics=("arbitrary", "arbitrary", "arbitrary"),
            vmem_limit_bytes=_vmem_limit(vmem_bytes(bm, bn))),
        name=name,
    )(*args)


def _norm_body(x_ref, g_ref, sh_ref, sc_ref, o_ref, *, n_x):
    bm = x_ref.shape[0]
    chunk = SUBLANES
    assert n_x % chunk == 0

    def rows_chunk(c, carry):
        r = pl.ds(pl.multiple_of(c * chunk, chunk), chunk)
        t = (pl.program_id(0) * bm + c * chunk >= n_x).astype(jnp.int32)
        x = x_ref[r, :]
        y = x * lax.rsqrt(jnp.mean(x * x, axis=-1, keepdims=True) + EPS)
        y = y * g_ref[...]
        o_ref[r, :] = (y * (1.0 + sc_ref[pl.ds(t, 1), :]) + sh_ref[pl.ds(t, 1), :]).astype(o_ref.dtype)
        return carry

    lax.fori_loop(0, bm // chunk, rows_chunk, 0, unroll=4)


def _norm_modulate(x, gain, mods, *, shift_col, scale_col, n_x):
    M, D = x.shape
    bm = _largest_divisor(M, (1024, 768, 512, 384, 256, 128, 64, 32, 16))
    nbytes = 2 * bm * D * 4 + 2 * bm * D * 2 + 6 * 8 * D * 4
    return pl.pallas_call(
        functools.partial(_norm_body, n_x=n_x),
        grid=(M // bm,),
        in_specs=[
            pl.BlockSpec((bm, D), lambda i: (i, 0)),
            pl.BlockSpec((1, D), lambda i: (0, 0)),
            pl.BlockSpec((mods.shape[0], D), lambda i: (0, shift_col)),
            pl.BlockSpec((mods.shape[0], D), lambda i: (0, scale_col)),
        ],
        out_specs=pl.BlockSpec((bm, D), lambda i: (i, 0)),
        out_shape=jax.ShapeDtypeStruct((M, D), BF16),
        compiler_params=pltpu.CompilerParams(
            dimension_semantics=("arbitrary",), vmem_limit_bytes=_vmem_limit(nbytes)),
        name="norm_modulate",
    )(x, gain.reshape(1, D), mods, mods)


def _rms_gain(xf, gain):
    return (xf * lax.rsqrt(jnp.mean(xf * xf, axis=-1, keepdims=True) + EPS)) * gain


def _softmax_pv(parts, extra_logit=None):
    m_tile = None
    for s, _ in parts:
        for t in range(s.shape[1] // LANES):
            st = s[:, t * LANES:(t + 1) * LANES]
            m_tile = st if m_tile is None else jnp.maximum(m_tile, st)
    m = jnp.max(m_tile, axis=-1, keepdims=True)
    if extra_logit is not None:
        m = jnp.maximum(m, extra_logit)
    acc = None
    for s, v in parts:
        v_ones = jnp.concatenate([v, jnp.ones_like(v)], axis=1)
        ov = jnp.dot(jnp.exp2(s - m).astype(BF16), v_ones, preferred_element_type=F32)
        acc = ov if acc is None else acc + ov
    d = acc.shape[1] // 2
    o, l = acc[:, :d], acc[:, d:]
    if extra_logit is not None:
        l = l + jnp.exp2(extra_logit - m)
    return o / l


def _na_bias_planes(rb_ref, tb_ref, kh):
    W, kw = GRID_W, NA_KW
    shape = (W, 2 * W)
    qc = lax.broadcasted_iota(jnp.int32, shape, 0)
    lane = lax.broadcasted_iota(jnp.int32, shape, 1)
    kc = lane % W
    win = jnp.clip(qc - kw // 2, 0, W - kw)
    in_window = (kc >= win) & (kc < win + kw)
    left = lane < W
    neg = jnp.full(shape, NEG_INF, F32)

    def toeplitz(d, lane0):
        if d < 0 or d > 2 * kh - 2:
            return neg
        row = jnp.broadcast_to(rb_ref[0, d:d + 1, :], shape)
        return pltpu.roll(row, (lane0 - (kw - 1)) % LANES, 1, stride=1, stride_axis=0)

    for p in range(2 * kh + 1):
        both = jnp.where(in_window, jnp.where(left, toeplitz(p - 1, 0), toeplitz(p, W)), NEG_INF)
        tb_ref[0, p] = both
        tb_ref[1, p] = jnp.where(left, both, NEG_INF)
        tb_ref[2, p] = jnp.where(left, NEG_INF, both)


def _na_body(q_ref, k_ref, v_ref, rb_ref, o_ref, tb_ref, sl_ref, sc_ref, *, n_x, n_ctx, rows, kh):
    W = GRID_W
    nkr = kh + 2
    n_steps = rows // 2
    kc = k_ref[n_x:n_x + n_ctx, :]
    vc = v_ref[n_x:n_x + n_ctx, :]
    _na_bias_planes(rb_ref, tb_ref, kh)

    def windows(i):
        r = 2 * i
        a = jnp.clip(r - kh // 2, 0, rows - nkr)
        return (r, a, pl.ds(pl.multiple_of(r * W, 2 * W), 2 * W),
                pl.ds(pl.multiple_of(a * W, 2 * W), nkr * W))

    def scores(i, slot):
        r, a, qrows, krows = windows(i)
        q2 = q_ref[qrows, :]
        s_loc = lax.dot_general(q2, k_ref[krows, :], _NT, preferred_element_type=F32)
        bias_rows = []
        for j in range(2):
            qr = r + j
            r0 = jnp.clip(qr - kh // 2, 0, rows - kh)
            pieces = []
            for t in range(nkr // 2):
                kr0 = a + 2 * t
                in0 = (kr0 >= r0) & (kr0 < r0 + kh)
                in1 = (kr0 + 1 >= r0) & (kr0 + 1 < r0 + kh)
                variant = jnp.where(in0 & in1, 0, jnp.where(in0, 1, 2))
                plane = jnp.where(in0 | in1, jnp.clip(kr0 - qr + kh, 0, 2 * kh - 1), 2 * kh)
                pieces.append(tb_ref[variant, plane])
            bias_rows.append(jnp.concatenate(pieces, axis=1))
        sl_ref[slot] = s_loc + jnp.concatenate(bias_rows, axis=0)
        sc_ref[slot] = lax.dot_general(q2, kc, _NT, preferred_element_type=F32)

    def attend(i, slot):
        _, _, qrows, krows = windows(i)
        o = _softmax_pv([(sl_ref[slot], v_ref[krows, :]), (sc_ref[slot], vc)])
        o_ref[qrows, :] = o.astype(o_ref.dtype)

    scores(0, 0)
    scores(1, 1)

    def pairs(u, carry):
        for ph in range(PAIRS_PER_BODY):
            t = 2 * (PAIRS_PER_BODY * u + ph)
            cur = 2 * (ph % 2)
            for j in range(2):
                scores(jnp.minimum(t + 2 + j, n_steps - 1), 2 - cur + j)
                attend(t + j, cur + j)
        return carry

    lax.fori_loop(0, n_steps // (2 * PAIRS_PER_BODY), pairs, 0)

    ctx_rows = slice(n_x, n_x + n_ctx)
    s = lax.dot_general(q_ref[ctx_rows, :], kc, _NT, preferred_element_type=F32)
    o_ref[ctx_rows, :] = _softmax_pv([(s, vc)]).astype(o_ref.dtype)

def _na_attention(qkv, rel_bias, *, n_x, n_ctx):
    mtot, width = qkv.shape
    H = width // (3 * HEAD_DIM)
    rows = n_x // GRID_W
    kh = min(NA_KH, rows)
    assert GRID_W * 2 == LANES and rows % (4 * PAIRS_PER_BODY) == 0 and rows >= kh + 2
    assert kh == NA_KH and kh % 2 == 0 and PAIRS_PER_BODY % 2 == 0 and SCORE_SLOTS == 4
    assert rel_bias.shape == (H, 2 * kh - 1, 2 * NA_KW - 1) and 2 * NA_KW - 1 <= LANES
    rb = jnp.pad(rel_bias.astype(F32) * LOG2E, ((0, 0), (0, 1), (0, LANES - (2 * NA_KW - 1))))
    blk = (mtot, HEAD_DIM)
    nq, nk = 2 * GRID_W, (kh + 2) * GRID_W
    plane_shape = (3, 2 * kh + 1, GRID_W, 2 * GRID_W)
    nbytes = (8 * mtot * HEAD_DIM * 2 + int(np.prod(plane_shape)) * 4
              + SCORE_SLOTS * nq * (nk + n_ctx) * 4)
    return pl.pallas_call(
        functools.partial(_na_body, n_x=n_x, n_ctx=n_ctx, rows=rows, kh=kh),
        grid=(H,),
        in_specs=[
            pl.BlockSpec(blk, lambda h: (0, h)),
            pl.BlockSpec(blk, lambda h: (0, H + h)),
            pl.BlockSpec(blk, lambda h: (0, 2 * H + h)),
            pl.BlockSpec((1,) + rb.shape[1:], lambda h: (h, 0, 0)),
        ],
        out_specs=pl.BlockSpec(blk, lambda h: (0, h)),
        out_shape=jax.ShapeDtypeStruct((mtot, H * HEAD_DIM), BF16),
        scratch_shapes=[pltpu.VMEM(plane_shape, F32), pltpu.VMEM((SCORE_SLOTS, nq, nk), F32),
                        pltpu.VMEM((SCORE_SLOTS, nq, n_ctx), F32)],
        compiler_params=pltpu.CompilerParams(
            dimension_semantics=("arbitrary",), vmem_limit_bytes=_vmem_limit(nbytes)),
        name="na_attention",
    )(qkv, qkv, qkv, rb)


def _wa_body(q_ref, k_ref, v_ref, ca_ref, sa_ref, cb_ref, sb_ref, band_ref, sink_ref,
             o_ref, kn_ref, sl_ref, sc_ref, *, n_x, n_ctx, group):
    W = WA_WINDOW
    rpb = W // GRID_W
    nblk = n_x // W
    g = pl.program_id(1)
    lane = lax.broadcasted_iota(jnp.int32, (W, HEAD_DIM), 1)
    neg_half = (lane % (HEAD_DIM // 2)) < HEAD_DIM // 4

    def rope(y, b):
        def table(row_ref, col_ref):
            row_part = [jnp.broadcast_to(row_ref[pl.ds(rpb * b + j, 1), :], (GRID_W, HEAD_DIM))
                        for j in range(rpb)]
            return jnp.concatenate(row_part, axis=0) + jnp.concatenate([col_ref[...]] * rpb, axis=0)
        rot = jnp.where(neg_half, -pltpu.roll(y, HEAD_DIM - HEAD_DIM // 4, 1),
                        pltpu.roll(y, HEAD_DIM // 4, 1))
        return y * table(ca_ref, cb_ref) + rot * table(sa_ref, sb_ref)

    @pl.when(g == 0)
    def _prepare_keys():
        def kblock(b, carry):
            r = pl.ds(pl.multiple_of(b * W, W), W)
            kn_ref[r, :] = rope(k_ref[r, :].astype(F32), b).astype(BF16)
            return carry
        lax.fori_loop(0, nblk, kblock, 0, unroll=2)
        kn_ref[n_x:n_x + n_ctx, :] = k_ref[n_x:n_x + n_ctx, :]

    sink = sink_ref[pl.program_id(0) * group + g] * LOG2E
    kc = kn_ref[n_x:n_x + n_ctx, :]
    vc = v_ref[n_x:n_x + n_ctx, :]

    def windows(b):
        a = jnp.clip((b - 1) * W, 0, n_x - 3 * W)
        return pl.ds(pl.multiple_of(b * W, W), W), pl.ds(pl.multiple_of(a, W), 3 * W)

    def scores(b, slot):
        r, krows = windows(b)
        qn = rope(q_ref[r, :].astype(F32), b).astype(BF16)
        s_loc = lax.dot_general(qn, kn_ref[krows, :], _NT, preferred_element_type=F32)
        edge = jnp.where(b == 0, 0, jnp.where(b == nblk - 1, 2, 1))
        sl_ref[slot] = s_loc + band_ref[edge]
        sc_ref[slot] = lax.dot_general(qn, kc, _NT, preferred_element_type=F32)

    def attend(b, slot):
        r, krows = windows(b)
        o = _softmax_pv([(sl_ref[slot], v_ref[krows, :]), (sc_ref[slot], vc)], extra_logit=sink)
        o_ref[r, :] = o.astype(o_ref.dtype)

    scores(0, 0)
    scores(1, 1)

    def pairs(u, carry):
        for ph in range(PAIRS_PER_BODY):
            t = 2 * (PAIRS_PER_BODY * u + ph)
            cur = 2 * (ph % 2)
            for j in range(2):
                scores(jnp.minimum(t + 2 + j, nblk - 1), 2 - cur + j)
                attend(t + j, cur + j)
        return carry

    lax.fori_loop(0, nblk // (2 * PAIRS_PER_BODY), pairs, 0)

    for c in range(n_ctx // W):
        r = slice(n_x + c * W, n_x + (c + 1) * W)
        s = lax.dot_general(q_ref[r, :], kc, _NT, preferred_element_type=F32)
        o_ref[r, :] = _softmax_pv([(s, vc)], extra_logit=sink).astype(o_ref.dtype)


def _rope_tables(rows):
    axis_dim = HEAD_DIM // 2
    inv = ROPE_BASE ** (-jnp.arange(0, axis_dim, 2, dtype=F32) / axis_dim)
    ar = jnp.arange(rows, dtype=F32)[:, None] * inv
    ac = jnp.arange(GRID_W, dtype=F32)[:, None] * inv
    zr = jnp.zeros((rows, axis_dim), F32)
    zc = jnp.zeros((GRID_W, axis_dim), F32)

    def row_tab(f):
        return jnp.concatenate([f(ar), f(ar), zr], axis=-1)

    def col_tab(f):
        return jnp.concatenate([zc, f(ac), f(ac)], axis=-1)

    return row_tab(jnp.cos), row_tab(jnp.sin), col_tab(jnp.cos), col_tab(jnp.sin)


def _wa_band_bias():
    W = WA_WINDOW
    dist = np.arange(3 * W)[None, :] - np.arange(W)[:, None]
    return np.stack([np.where(np.abs(dist + d) <= W, 0.0, NEG_INF) for d in (0, -W, -2 * W)]
                    ).astype(np.float32)


def _wa_attention(qkv, sink, *, n_x, n_ctx, n_heads):
    mtot = qkv.shape[0]
    H, KVH = n_heads, WA_KV_HEADS
    G = H // KVH
    W = WA_WINDOW
    assert W % GRID_W == 0 and n_x % (2 * PAIRS_PER_BODY * W) == 0 and n_ctx % W == 0
    assert W == HEAD_DIM and PAIRS_PER_BODY % 2 == 0 and SCORE_SLOTS == 4
    ca, sa, cb, sb = _rope_tables(n_x // GRID_W)
    band = jnp.asarray(_wa_band_bias())
    blk = (mtot, HEAD_DIM)
    full = lambda arr: pl.BlockSpec(arr.shape, lambda kv, g: (0,) * arr.ndim)
    nbytes = (8 * mtot * HEAD_DIM * 2 + mtot * HEAD_DIM * 2 + 4 * (ca.size + cb.size) * 4
              + 2 * band.size * 4 + SCORE_SLOTS * W * (3 * W + n_ctx) * 4)
    return pl.pallas_call(
        functools.partial(_wa_body, n_x=n_x, n_ctx=n_ctx, group=G),
        grid=(KVH, G),
        in_specs=[
            pl.BlockSpec(blk, lambda kv, g: (0, kv * G + g)),
            pl.BlockSpec(blk, lambda kv, g: (0, H + kv)),
            pl.BlockSpec(blk, lambda kv, g: (0, H + KVH + kv)),
            full(ca), full(sa), full(cb), full(sb), full(band),
            pl.BlockSpec(memory_space=pltpu.SMEM),
        ],
        out_specs=pl.BlockSpec(blk, lambda kv, g: (0, kv * G + g)),
        out_shape=jax.ShapeDtypeStruct((mtot, H * HEAD_DIM), BF16),
        scratch_shapes=[pltpu.VMEM(blk, BF16), pltpu.VMEM((SCORE_SLOTS, W, 3 * W), F32),
                        pltpu.VMEM((SCORE_SLOTS, W, n_ctx), F32)],
        compiler_params=pltpu.CompilerParams(
            dimension_semantics=("arbitrary", "arbitrary"), vmem_limit_bytes=_vmem_limit(nbytes)),
        name="wa_attention",
    )(qkv, qkv, qkv, ca, sa, cb, sb, band, sink.astype(F32))


def _sg_body(u_ref, v_ref, vg_ref, ws_ref, bs_ref, o_ref, *, n_chunks, groups):
    P = SG_CHUNK
    gw = v_ref.shape[1] // groups
    vn = _rms_gain(v_ref[...].astype(F32), vg_ref[...]).astype(BF16)
    for g in range(groups):
        wg = ws_ref[g].astype(BF16)
        cols = slice(g * gw, (g + 1) * gw)
        for c in range(n_chunks):
            rows = slice(c * P, (c + 1) * P)
            sv = jnp.dot(wg, vn[rows, cols], preferred_element_type=F32) + bs_ref[g]
            o_ref[rows, cols] = (u_ref[rows, cols].astype(F32) * sv).astype(o_ref.dtype)


def _spatial_gate(uv, v_gain, w_s, b_s):
    mtot, two_d = uv.shape
    D = two_d // 2
    G, P = SG_GROUPS, SG_CHUNK
    gw = D // G
    assert gw % LANES == 0 and mtot % P == 0
    n_chunks = _largest_divisor(mtot // P, (3, 2, 1))
    bm = n_chunks * P
    bias = jnp.broadcast_to(b_s.astype(F32)[:, :, None], (G, P, gw))
    nbytes = 6 * bm * D * 2 + bm * D * 6 + 2 * (w_s.size + bias.size) * 4
    return pl.pallas_call(
        functools.partial(_sg_body, n_chunks=n_chunks, groups=G),
        grid=(mtot // bm,),
        in_specs=[
            pl.BlockSpec((bm, D), lambda i: (i, 0)),
            pl.BlockSpec((bm, D), lambda i: (i, 1)),
            pl.BlockSpec((1, D), lambda i: (0, 0)),
            pl.BlockSpec(w_s.shape, lambda i: (0, 0, 0)),
            pl.BlockSpec(bias.shape, lambda i: (0, 0, 0)),
        ],
        out_specs=pl.BlockSpec((bm, D), lambda i: (i, 0)),
        out_shape=jax.ShapeDtypeStruct((mtot, D), BF16),
        compiler_params=pltpu.CompilerParams(
            dimension_semantics=("arbitrary",), vmem_limit_bytes=_vmem_limit(nbytes)),
        name="spatial_gate",
    )(uv, uv, v_gain.reshape(1, D), w_s, bias)


def _head_gains(q_gain, k_gain):
    scale = np.float32(HEAD_DIM ** -0.5) * LOG2E
    rows = jnp.stack([q_gain.astype(F32) * scale, k_gain.astype(F32), jnp.ones((HEAD_DIM,), F32)])
    return jnp.concatenate([rows, jnp.zeros((5, HEAD_DIM), F32)], axis=0)


def _ada_modulation(cond, a, b, bias, layer):
    t = _matmul(cond, a, layer, out_dtype=F32, prologue="silu", name="ada_down")
    return _matmul(t, b, layer, out_dtype=F32, epilogue="bias", vec=bias[:, None, :],
                   bn=_largest_divisor(b.shape[2], (2048, 1024, 512, 256, 128)), name="ada_up")


def kernel(x, c, ctx, c_ctx, ada_a, ada_b, ada_bias, norm_mix, norm_mlp, mlp_w1, mlp_w2,
           na_w_qkv, na_w_o, na_q_norm, na_k_norm, na_rel_bias,
           wa_w_qkv, wa_w_o, wa_q_norm, wa_k_norm, wa_sink,
           sg_w_in, sg_v_norm, sg_w_s, sg_b_s, sg_w_out):
    assert x.shape[0] == 1 and ctx.shape[0] == 1 and c.shape[0] == 1
    n_x, D = x.shape[1], x.shape[2]
    n_ctx = ctx.shape[1]
    depth = ada_a.shape[0]
    n_heads = D // HEAD_DIM
    n_mixers = 3

    xc = jnp.concatenate([x[0], ctx[0]], axis=0)
    cond = jnp.zeros((16, D), F32).at[0].set(c[0]).at[1].set(c_ctx)

    for i in range(depth):
        last = i == depth - 1
        kind, j = i % n_mixers, i // n_mixers
        mods = _ada_modulation(cond, ada_a, ada_b, ada_bias, i)
        gate = dict(epilogue="gate_res", vec=mods, n_x=n_x, out_dtype=F32,
                    m_rows=n_x if last else None)

        h = _norm_modulate(xc, norm_mix[i], mods, shift_col=0, scale_col=1, n_x=n_x)
        if kind == 0:
            qkv = _matmul(h, na_w_qkv, j, out_dtype=BF16, epilogue="head_norm",
                          vec=_head_gains(na_q_norm[j], na_k_norm[j]), norm_cols=(D, 2 * D),
                          name="na_qkv")
            o = _na_attention(qkv, na_rel_bias[j], n_x=n_x, n_ctx=n_ctx)
            xc = _matmul(o, na_w_o, j, res=xc, vec_col=2, name="na_out", **gate)
        elif kind == 1:
            qkv = _matmul(h, wa_w_qkv, j, out_dtype=BF16, epilogue="head_norm",
                          vec=_head_gains(wa_q_norm[j], wa_k_norm[j]),
                          norm_cols=(D, D + WA_KV_HEADS * HEAD_DIM), name="wa_qkv")
            o = _wa_attention(qkv, wa_sink[j], n_x=n_x, n_ctx=n_ctx, n_heads=n_heads)
            xc = _matmul(o, wa_w_o, j, res=xc, vec_col=2, name="wa_out", **gate)
        else:
            uv = _matmul(h, sg_w_in, j, out_dtype=BF16, epilogue="gelu", name="sg_in")
            z = _spatial_gate(uv, sg_v_norm[j], sg_w_s[j], sg_b_s[j])
            xc = _matmul(z, sg_w_out, j, res=xc, vec_col=2, name="sg_out", **gate)

        h2 = _norm_modulate(xc, norm_mlp[i], mods, shift_col=3, scale_col=4, n_x=n_x)
        hid = _matmul(h2, mlp_w1, i, out_dtype=BF16, epilogue="relu2", name="mlp_up")
        xc = _matmul(hid, mlp_w2, i, res=xc, vec_col=5, name="mlp_down", **gate)
    return xc[None]
```
